```python
import math
import jax
import jax.numpy as jnp
from jax import lax
import numpy as np

D_MODEL = 2048
BATCH = 1
SEQ = 16384
DEPTH = 1
DEC_BATCH = 32
DEC_SEQ = 1
PAST_LEN = 16384
PAGE_SIZE = 128

NSA_HEADS = 16
NSA_GROUPS = 4
NSA_REP = NSA_HEADS // NSA_GROUPS
HEAD_DIM = 64
CMP_BLOCK = 32
CMP_STRIDE = 16
CMP_HIDDEN = 64
SEL_BLOCK = 64
SEL_TOPN = 16
WINDOW = 512
SB_HEADS = 8
SB_HEAD_DIM = 128
SB_KBLOCK = 128
REL_BUCKETS = 32
REL_MAX_DIST = 4096
N_EXPERTS = 32
TOP_K = 4
D_FF = D_MODEL // 4
SWIGLU_LIMIT = 7.0
SWIGLU_ALPHA = 1.702
LN_EPS = 1e-5
Q_BLOCK = 128
NEG_INF = -1e30
FORCED_SCORE = 1e4
DEEPNORM_ALPHA = (2 * DEPTH) ** 0.25
DEEPNORM_BETA = (8 * DEPTH) ** -0.25

NSA_WIDTH = NSA_HEADS * HEAD_DIM
SB_WIDTH = SB_HEADS * SB_HEAD_DIM
NSA_KV_WIDTH = 2 * NSA_GROUPS * HEAD_DIM
IN_SPLITS = (NSA_WIDTH, NSA_KV_WIDTH, NSA_KV_WIDTH, NSA_KV_WIDTH, 3 * NSA_HEADS, SB_WIDTH, 2 * SB_WIDTH, 2 * D_MODEL)
IN_WIDTH = sum(IN_SPLITS)

kernel_name = 'nsa_stickbreaking_moe_deepnorm_step'


def layer_norm(x, g, b):
    xf = x.astype(jnp.float32)
    mu = jnp.mean(xf, -1, keepdims=True)
    var = jnp.mean(jnp.square(xf - mu), -1, keepdims=True)
    y = (xf - mu) * lax.rsqrt(var + LN_EPS) * g.astype(jnp.float32) + b.astype(jnp.float32)
    return y.astype(x.dtype)


def t5_bucket(dist):
    n = jnp.maximum(dist, 0)
    max_exact = REL_BUCKETS // 2
    nf = jnp.maximum(n, 1).astype(jnp.float32)
    large = max_exact + (jnp.log(nf / max_exact) / math.log(REL_MAX_DIST / max_exact)
                         * (REL_BUCKETS - max_exact)).astype(jnp.int32)
    return jnp.where(n < max_exact, n, jnp.minimum(large, REL_BUCKETS - 1))


def masked_softmax(s, mask):
    s = jnp.where(mask, s, NEG_INF)
    p = jnp.where(mask, jnp.exp(s - jnp.max(s, -1, keepdims=True)), 0.0)
    return p / jnp.maximum(jnp.sum(p, -1, keepdims=True), 1e-30)


def compress_kv(kv, pe, w1, w2):
    b, l = kv.shape[:2]
    nc = -(-l // CMP_STRIDE)
    kv = jnp.pad(kv, ((0, 0), (0, nc * CMP_STRIDE - l), (0, 0), (0, 0), (0, 0)))
    chunks = kv.reshape(b, nc, CMP_STRIDE, 2, NSA_GROUPS, HEAD_DIM)
    halves = CMP_BLOCK // CMP_STRIDE
    w1h = w1.reshape(2, halves, CMP_STRIDE, HEAD_DIM, CMP_HIDDEN)
    y = jnp.einsum('bcpkgd,khpdf->bchkgf', chunks, w1h)
    pe_bias = jnp.einsum('kpd,kpdf->kf', pe, w1)
    n_blk = nc - halves + 1
    pre = sum(y[:, h:h + n_blk, h] for h in range(halves)) + pe_bias[:, None, :]
    return jnp.einsum('bikgf,kfd->bikgd', jax.nn.gelu(pre), w2)


def to_sel_blocks(kv):
    b, l = kv.shape[:2]
    n = -(-l // SEL_BLOCK)
    kv = jnp.pad(kv, ((0, 0), (0, n * SEL_BLOCK - l), (0, 0), (0, 0), (0, 0)))
    return kv.reshape(b, n, SEL_BLOCK, 2, NSA_GROUPS, HEAD_DIM).transpose(0, 4, 1, 2, 3, 5)


def cmp_sel_overlap(n_cmp, n_sel):
    cs = jnp.arange(n_cmp) * CMP_STRIDE
    ss = jnp.arange(n_sel) * SEL_BLOCK
    ov = jnp.minimum(cs[:, None] + CMP_BLOCK, ss[None, :] + SEL_BLOCK) - jnp.maximum(cs[:, None], ss[None, :])
    return jnp.maximum(ov, 0).astype(jnp.float32) / CMP_BLOCK


def nsa_attend(q, gates, kv_cmp, kv_sel_blocks, kv_win, win_pos, q_pos, rel_bias):
    f32 = jnp.float32
    scale = HEAD_DIM ** -0.5
    b, tq = q.shape[:2]
    table_gr = rel_bias.reshape(REL_BUCKETS, NSA_GROUPS, NSA_REP)
    n_cmp = kv_cmp.shape[1]
    cmp_end = jnp.arange(n_cmp) * CMP_STRIDE + CMP_BLOCK - 1
    d_c = q_pos[:, None] - cmp_end[None, :]
    bias_c = jnp.transpose(table_gr[t5_bucket(d_c)], (0, 2, 3, 1))
    s_c = jnp.einsum('bqgrd,bigd->bqgri', q, kv_cmp[:, :, 0]).astype(f32) * scale + bias_c
    p_c = masked_softmax(s_c, (d_c >= 0)[:, None, None, :])
    o_c = jnp.einsum('bqgri,bigd->bqgrd', p_c.astype(q.dtype), kv_cmp[:, :, 1])
    n_sb = kv_sel_blocks.shape[2]
    k_top = min(SEL_TOPN, n_sb)
    imp = jnp.einsum('bqgri,ij->bqgj', p_c, cmp_sel_overlap(n_cmp, n_sb))
    blk = jnp.arange(n_sb)[None, :]
    qb = (q_pos // SEL_BLOCK)[:, None]
    valid_b = (blk <= qb)[:, None, :]
    forced = ((blk == 0) | (blk == qb) | (blk == qb - 1))[:, None, :]
    score = jnp.where(forced, FORCED_SCORE, jnp.where(valid_b, imp, -1.0))
    _, idx = lax.top_k(score, k_top)
    b_i = jnp.arange(b)[:, None, None, None]
    g_i = jnp.arange(NSA_GROUPS)[None, None, :, None]
    kv_s = kv_sel_blocks[b_i, g_i, idx]
    s_pos = idx[..., None] * SEL_BLOCK + jnp.arange(SEL_BLOCK)
    d_s = q_pos[None, :, None, None, None] - s_pos
    bias_s = jnp.moveaxis(table_gr[t5_bucket(d_s), g_i[..., None], :], -1, 3)
    s_s = jnp.einsum('bqgrd,bqgksd->bqgrks', q, kv_s[..., 0, :]).astype(f32) * scale + bias_s
    p_s = masked_softmax(s_s.reshape(b, tq, NSA_GROUPS, NSA_REP, -1),
                         (d_s >= 0).reshape(b, tq, NSA_GROUPS, 1, -1))
    o_s = jnp.einsum('bqgrn,bqgnd->bqgrd', p_s.astype(q.dtype),
                     kv_s[..., 1, :].reshape(b, tq, NSA_GROUPS, k_top * SEL_BLOCK, HEAD_DIM))
    d_w = q_pos[:, None] - win_pos[None, :]
    m_w = (d_w >= 0) & (d_w < WINDOW) & (win_pos >= 0)[None, :]
    bias_w = jnp.transpose(table_gr[t5_bucket(d_w)], (0, 2, 3, 1))
    s_w = jnp.einsum('bqgrd,bkgd->bqgrk', q, kv_win[:, :, 0]).astype(f32) * scale + bias_w
    p_w = masked_softmax(s_w, m_w[:, None, None, :])
    o_w = jnp.einsum('bqgrk,bkgd->bqgrd', p_w.astype(q.dtype), kv_win[:, :, 1])
    o = (gates[:, :, 0, :, :, None] * o_c + gates[:, :, 1, :, :, None] * o_s
         + gates[:, :, 2, :, :, None] * o_w)
    return o.reshape(b, tq, NSA_WIDTH)


def stick_breaking_attend(q, kv, q_pos):
    f32 = jnp.float32
    b, l = kv.shape[:2]
    tq = q.shape[1]
    nkb = -(-l // SB_KBLOCK)
    lp = nkb * SB_KBLOCK
    kv = jnp.pad(kv, ((0, 0), (0, lp - l), (0, 0), (0, 0), (0, 0)))
    z = (jnp.einsum('bqhd,bkhd->bhqk', q, kv[:, :, 0]).astype(f32) * SB_HEAD_DIM ** -0.5
         ).reshape(b, SB_HEADS, tq, nkb, SB_KBLOCK)
    k_pos = jnp.arange(lp).reshape(nkb, SB_KBLOCK)
    causal = k_pos[None] < q_pos[:, None, None]
    log_stay = jnp.where(causal, jax.nn.log_sigmoid(-z), 0.0)
    i = jnp.arange(SB_KBLOCK)
    after_in_blk = (i[:, None] > i[None, :]).astype(f32)
    n = jnp.arange(nkb)
    after_blk = (n[:, None] > n[None, :]).astype(f32)
    hi = lax.Precision.HIGHEST
    within = jnp.einsum('bhqnj,js->bhqns', log_stay, after_in_blk, precision=hi)
    later_blocks = jnp.einsum('bhqm,mn->bhqn', jnp.sum(log_stay, -1), after_blk, precision=hi)
    a = jnp.where(causal, jnp.exp(jax.nn.log_sigmoid(z) + within + later_blocks[..., None]), 0.0)
    a = a.reshape(b, SB_HEADS, tq, lp)
    return jnp.einsum('bhqk,bkhd->bqhd', a.astype(q.dtype), kv[:, :, 1])


def project(x, w_in_l):
    b, t = x.shape[:2]
    points = [int(v) for v in np.cumsum(IN_SPLITS)[:-1]]
    q_a, kv_c, kv_s, kv_w, g_a, q_b, kv_b, g_m = jnp.split(x @ w_in_l, points, axis=-1)
    nsa_kv = lambda a: a.reshape(b, t, 2, NSA_GROUPS, HEAD_DIM)
    return (q_a.reshape(b, t, NSA_GROUPS, NSA_REP, HEAD_DIM), nsa_kv(kv_c), nsa_kv(kv_s), nsa_kv(kv_w),
            jax.nn.sigmoid(g_a).reshape(b, t, 3, NSA_GROUPS, NSA_REP),
            q_b.reshape(b, t, SB_HEADS, SB_HEAD_DIM), kv_b.reshape(b, t, 2, SB_HEADS, SB_HEAD_DIM), g_m)


def prompt_mixers(q_a, kv_c, kv_s, kv_w, g_a, q_b, kv_b, cmp_pe_l, cmp_w1_l, cmp_w2_l, rel_bias):
    b, t = q_a.shape[:2]
    kv_cmp = compress_kv(kv_c, cmp_pe_l, cmp_w1_l, cmp_w2_l)
    sel_blocks = to_sel_blocks(kv_s)
    kv_w_pad = jnp.pad(kv_w, ((0, 0), (WINDOW, 0), (0, 0), (0, 0), (0, 0)))

    def nsa_block(c):
        s = c * Q_BLOCK
        qp = s + jnp.arange(Q_BLOCK)
        qa = lax.dynamic_slice_in_dim(q_a, s, Q_BLOCK, axis=1)
        ga = lax.dynamic_slice_in_dim(g_a, s, Q_BLOCK, axis=1)
        kw = lax.dynamic_slice_in_dim(kv_w_pad, s, WINDOW + Q_BLOCK, axis=1)
        wp = s - WINDOW + jnp.arange(WINDOW + Q_BLOCK)
        return nsa_attend(qa, ga, kv_cmp, sel_blocks, kw, wp, qp, rel_bias)

    o_a = lax.map(nsa_block, jnp.arange(t // Q_BLOCK))
    o_a = jnp.moveaxis(o_a, 0, 1).reshape(b, t, NSA_WIDTH)
    o_b = []
    for c in range(t // Q_BLOCK):
        s, e = c * Q_BLOCK, (c + 1) * Q_BLOCK
        o_b.append(stick_breaking_attend(q_b[:, s:e], kv_b[:, :e], jnp.arange(s, e)))
    o_b = jnp.concatenate(o_b, axis=1).reshape(b, t, SB_WIDTH)
    return o_a, o_b


def sample_mixers(l, page_table, q_a, kv_c, kv_s, kv_w, g_a, q_b, kv_b, win_buf,
                  cache_cmp_kv, cache_sel_kv, cache_sb_kv, cmp_pe_l, cmp_w1_l, cmp_w2_l, rel_bias):
    n_new = q_a.shape[1]
    past = page_table.shape[1] * PAGE_SIZE
    n_buf = win_buf.shape[1]
    q_pos = past + jnp.arange(n_new)
    win_pos = past - n_buf + jnp.arange(n_buf + n_new)

    def with_past(cache, pages, new):
        rows = cache[l, pages].reshape(past, *cache.shape[3:])
        return jnp.concatenate([rows, new], axis=0)[None]

    def one(args):
        pages, qa, kc, ks, kw, ga, qb, kb, wb = args
        o_a = nsa_attend(qa[None], ga[None],
                         compress_kv(with_past(cache_cmp_kv, pages, kc), cmp_pe_l, cmp_w1_l, cmp_w2_l),
                         to_sel_blocks(with_past(cache_sel_kv, pages, ks)),
                         jnp.concatenate([wb, kw], axis=0)[None], win_pos, q_pos, rel_bias)
        o_b = stick_breaking_attend(qb[None], with_past(cache_sb_kv, pages, kb), q_pos)
        return o_a[0], o_b[0].reshape(n_new, SB_WIDTH)

    return lax.map(one, (page_table, q_a, kv_c, kv_s, kv_w, g_a, q_b, kv_b, win_buf))


def moe_ffn(x, w_router, b_router, w_gate_up, b_gate_up, w_down, b_down):
    logits = (x @ w_router).astype(jnp.float32) + b_router.astype(jnp.float32)
    top_vals, top_idx = lax.top_k(logits, TOP_K)
    top_w = jax.nn.softmax(top_vals, axis=-1)
    combine = jnp.sum(jax.nn.one_hot(top_idx, N_EXPERTS, dtype=jnp.float32) * top_w[..., None],
                      axis=-2).astype(x.dtype)
    out = jnp.zeros_like(x)
    for e in range(N_EXPERTS):
        h = x @ w_gate_up[e] + b_gate_up[e]
        gate = jnp.minimum(h[:, 0::2], SWIGLU_LIMIT)
        up = jnp.clip(h[:, 1::2], -SWIGLU_LIMIT, SWIGLU_LIMIT)
        act = (up + 1.0) * gate * jax.nn.sigmoid(SWIGLU_ALPHA * gate)
        out = out + combine[:, e:e + 1] * (act @ w_down[e] + b_down[e])
    return out


def finish(x, o_a, o_b, g_m, w_up_a, w_up_b, w_o, ln1_g, ln1_b,
           w_router, b_router, w_gate_up, b_gate_up, w_down, b_down, ln2_g, ln2_b):
    g_a, g_b = jnp.split(jax.nn.sigmoid(g_m), 2, axis=-1)
    mixed = (g_a * (o_a @ w_up_a) + g_b * (o_b @ w_up_b)) @ w_o
    h = layer_norm(DEEPNORM_ALPHA * x + mixed, ln1_g, ln1_b)
    f = moe_ffn(h.reshape(-1, D_MODEL), w_router, b_router, w_gate_up, b_gate_up, w_down, b_down)
    return layer_norm(DEEPNORM_ALPHA * h + f.reshape(h.shape), ln2_g, ln2_b)


def setup_inputs(seed: int = 0) -> dict:
    key = jax.random.key(seed)
    ks = jax.random.split(key, 26)
    f32 = jnp.float32
    n_pages = PAST_LEN // PAGE_SIZE
    n_used = DEC_BATCH * n_pages
    n_pool = n_used + max(1, n_used // 4)
    win_buf = min(WINDOW, PAST_LEN)
    nrm = lambda k, shape, s: jax.random.normal(k, shape, f32) * s
    col_scale = np.ones(IN_WIDTH, np.float32)
    offs = np.cumsum((0,) + IN_SPLITS)
    for i in (1, 2, 3):
        col_scale[offs[i] + NSA_KV_WIDTH // 2: offs[i + 1]] = DEEPNORM_BETA
    col_scale[offs[6] + SB_WIDTH: offs[7]] = DEEPNORM_BETA
    return {
        'x_prompt': jax.random.normal(ks[0], (BATCH, SEQ, D_MODEL), f32),
        'x_sample': jax.random.normal(ks[1], (DEC_BATCH, DEC_SEQ, D_MODEL), f32),
        'cache_cmp_kv': jax.random.normal(ks[2], (DEPTH, n_pool, PAGE_SIZE, 2, NSA_GROUPS, HEAD_DIM), f32),
        'cache_sel_kv': jax.random.normal(ks[3], (DEPTH, n_pool, PAGE_SIZE, 2, NSA_GROUPS, HEAD_DIM), f32),
        'cache_sb_kv': jax.random.normal(ks[4], (DEPTH, n_pool, PAGE_SIZE, 2, SB_HEADS, SB_HEAD_DIM), f32),
        'state_win_kv': jax.random.normal(ks[5], (DEPTH, DEC_BATCH, win_buf, 2, NSA_GROUPS, HEAD_DIM), f32),
        'page_table': jax.random.permutation(ks[6], n_pool)[:n_used].reshape(DEC_BATCH, n_pages).astype(jnp.int32),
        'rel_bias': nrm(ks[7], (REL_BUCKETS, NSA_HEADS), 0.5),
        'w_in': nrm(ks[8], (DEPTH, D_MODEL, IN_WIDTH), D_MODEL ** -0.5) * jnp.asarray(col_scale),
        'cmp_pe': nrm(ks[9], (DEPTH, 2, CMP_BLOCK, HEAD_DIM), 0.1),
        'cmp_w1': nrm(ks[10], (DEPTH, 2, CMP_BLOCK, HEAD_DIM, CMP_HIDDEN), (CMP_BLOCK * HEAD_DIM) ** -0.5),
        'cmp_w2': nrm(ks[11], (DEPTH, 2, CMP_HIDDEN, HEAD_DIM), CMP_HIDDEN ** -0.5),
        'w_up_nsa': nrm(ks[12], (DEPTH, NSA_WIDTH, D_MODEL), NSA_WIDTH ** -0.5),
        'w_up_sb': nrm(ks[13], (DEPTH, SB_WIDTH, D_MODEL), SB_WIDTH ** -0.5),
        'w_out': nrm(ks[14], (DEPTH, D_MODEL, D_MODEL), D_MODEL ** -0.5 * DEEPNORM_BETA),
        'ln1_g': 1.0 + nrm(ks[15], (DEPTH, D_MODEL), 0.02),
        'ln1_b': nrm(ks[16], (DEPTH, D_MODEL), 0.02),
        'w_router': nrm(ks[17], (DEPTH, D_MODEL, N_EXPERTS), D_MODEL ** -0.5),
        'b_router': nrm(ks[18], (DEPTH, N_EXPERTS), 0.01),
        'w_gate_up': nrm(ks[19], (DEPTH, N_EXPERTS, D_MODEL, 2 * D_FF), D_MODEL ** -0.5 * DEEPNORM_BETA),
        'b_gate_up': nrm(ks[20], (DEPTH, N_EXPERTS, 2 * D_FF), 0.01),
        'w_down': nrm(ks[21], (DEPTH, N_EXPERTS, D_FF, D_MODEL), D_FF ** -0.5 * DEEPNORM_BETA),
        'b_down': nrm(ks[22], (DEPTH, N_EXPERTS, D_MODEL), 0.01),
        'ln2_g': 1.0 + nrm(ks[23], (DEPTH, D_MODEL), 0.02),
        'ln2_b': nrm(ks[24], (DEPTH, D_MODEL), 0.02),
    }


def reference(x_prompt, x_sample, cache_cmp_kv, cache_sel_kv, cache_sb_kv, state_win_kv, page_table, rel_bias,
              w_in, cmp_pe, cmp_w1, cmp_w2, w_up_nsa, w_up_sb, w_out, ln1_g, ln1_b,
              w_router, b_router, w_gate_up, b_gate_up, w_down, b_down, ln2_g, ln2_b):
    hp, hs = x_prompt, x_sample
    cmp_p, sel_p, sb_p, win_p = [], [], [], []
    cmp_s, sel_s, sb_s, win_s = [], [], [], []
    for l in range(DEPTH):
        lw = (w_up_nsa[l], w_up_sb[l], w_out[l], ln1_g[l], ln1_b[l], w_router[l], b_router[l],
              w_gate_up[l], b_gate_up[l], w_down[l], b_down[l], ln2_g[l], ln2_b[l])
        q_a, kv_c, kv_s, kv_w, g_a, q_b, kv_b, g_m = project(hp, w_in[l])
        o_a, o_b = prompt_mixers(q_a, kv_c, kv_s, kv_w, g_a, q_b, kv_b, cmp_pe[l], cmp_w1[l], cmp_w2[l], rel_bias)
        cmp_p.append(kv_c)
        sel_p.append(kv_s)
        sb_p.append(kv_b)
        win_p.append(kv_w[:, kv_w.shape[1] - min(WINDOW, kv_w.shape[1]):])
        hp = finish(hp, o_a, o_b, g_m, *lw)
        q_a, kv_c, kv_s, kv_w, g_a, q_b, kv_b, g_m = project(hs, w_in[l])
        o_a, o_b = sample_mixers(l, page_table, q_a, kv_c, kv_s, kv_w, g_a, q_b, kv_b, state_win_kv[l],
                                 cache_cmp_kv, cache_sel_kv, cache_sb_kv, cmp_pe[l], cmp_w1[l], cmp_w2[l], rel_bias)
        n_buf = state_win_kv.shape[2]
        cmp_s.append(kv_c)
        sel_s.append(kv_s)
        sb_s.append(kv_b)
        win_all = jnp.concatenate([state_win_kv[l], kv_w], axis=1)
        win_s.append(win_all[:, win_all.shape[1] - n_buf:])
        hs = finish(hs, o_a, o_b, g_m, *lw)
    return (hp, hs, jnp.stack(cmp_p), jnp.stack(sel_p), jnp.stack(sb_p), jnp.stack(win_p),
            jnp.stack(cmp_s), jnp.stack(sel_s), jnp.stack(sb_s), jnp.stack(win_s))
```

```python
import functools
import math

import numpy as np
import jax
import jax.numpy as jnp
from jax import lax
from jax.experimental import pallas as pl
from jax.experimental.pallas import tpu as pltpu

F32 = jnp.float32
BF16 = jnp.bfloat16

NSA_HEADS = 16
NSA_GROUPS = 4
NSA_REP = NSA_HEADS // NSA_GROUPS
HEAD_DIM = 64
CMP_BLOCK = 32
CMP_STRIDE = 16
CMP_HIDDEN = 64
SEL_BLOCK = 64
SEL_TOPN = 16
WINDOW = 512
SB_HEADS = 8
SB_HEAD_DIM = 128
REL_BUCKETS = 32
REL_MAX_DIST = 4096
TOP_K = 4
SWIGLU_LIMIT = 7.0
SWIGLU_ALPHA = 1.702
LN_EPS = 1e-5
NEG_INF = -1e30
FORCED_SCORE = 1e4
PAGE_SIZE = 128

LANES = 128
KEY_TILE = 128
NSA_WIDTH = NSA_HEADS * HEAD_DIM
SB_WIDTH = SB_HEADS * SB_HEAD_DIM
NSA_KV_WIDTH = 2 * NSA_GROUPS * HEAD_DIM
CHUNK_WIDTH = CMP_STRIDE * NSA_KV_WIDTH
CMP_PRE_WIDTH = 2 * NSA_GROUPS * CMP_HIDDEN
VMEM_LIMIT = 56 * 1024 * 1024
PAGES_PER_STEP = 8
EXP_UNDERFLOW = -104.0

_MAX_EXACT = REL_BUCKETS // 2
_T5_THRESH = tuple(int(math.ceil(_MAX_EXACT * 2.0 ** (j / 2.0) - 1e-9)) for j in range(1, REL_BUCKETS - _MAX_EXACT))
_FAR_DIST = _T5_THRESH[-1]
N_BIAS_TILES = -(-(_FAR_DIST + KEY_TILE) // KEY_TILE) + 1

_OFF_QA = 0
_OFF_KVC = _OFF_QA + NSA_WIDTH
_OFF_KVS = _OFF_KVC + NSA_KV_WIDTH
_OFF_KVW = _OFF_KVS + NSA_KV_WIDTH
_OFF_GA = _OFF_KVW + NSA_KV_WIDTH
_OFF_QB = _OFF_GA + LANES
_OFF_KVB = _OFF_QB + SB_WIDTH
_OFF_END = _OFF_KVB + 2 * SB_WIDTH
PROJ_TN = 768


def _cparams(sem):
    return pltpu.CompilerParams(dimension_semantics=sem, vmem_limit_bytes=VMEM_LIMIT)


def _round_up(x, m):
    return -(-x // m) * m


def _dot(a, b):
    return jnp.dot(a, b, preferred_element_type=F32)


def _dot_nt(a, b):
    return lax.dot_general(a, b, (((1,), (1,)), ((), ())), preferred_element_type=F32)


def _layer_norm(x, g, b):
    mu = jnp.mean(x, axis=-1, keepdims=True)
    xc = x - mu
    var = jnp.mean(xc * xc, axis=-1, keepdims=True)
    return xc * lax.rsqrt(var + LN_EPS) * g + b


def _matmul_kernel(x_ref, w_ref, o_ref, xb_ref):
    @pl.when(pl.program_id(1) == 0)
    def _():
        xb_ref[...] = x_ref[...].astype(BF16)

    o_ref[...] = _dot(xb_ref[...], w_ref[...]).astype(o_ref.dtype)


def _matmul(x, w, tm, tn):
    m, k = x.shape
    n = w.shape[1]
    return pl.pallas_call(
        _matmul_kernel,
        grid=(m // tm, n // tn),
        in_specs=[pl.BlockSpec((tm, k), lambda i, j: (i, 0)),
                  pl.BlockSpec((k, tn), lambda i, j: (0, j))],
        out_specs=pl.BlockSpec((tm, tn), lambda i, j: (i, j)),
        out_shape=jax.ShapeDtypeStruct((m, n), F32),
        scratch_shapes=[pltpu.VMEM((tm, k), BF16)],
        compiler_params=_cparams(("arbitrary", "arbitrary")),
        name="in_proj",
    )(x, w)


def _proj_weight(w_in_l, d_model):
    o = np.cumsum((0, NSA_WIDTH, NSA_KV_WIDTH, NSA_KV_WIDTH, NSA_KV_WIDTH, 3 * NSA_HEADS,
                   SB_WIDTH, 2 * SB_WIDTH, 2 * d_model))
    g_m = w_in_l[:, o[7]:o[8]]
    head = w_in_l[:, o[0]:o[5]]
    tail = w_in_l[:, o[5]:o[7]]
    pad_a = jnp.zeros((d_model, LANES - 3 * NSA_HEADS), w_in_l.dtype)
    n = 2 * d_model + _OFF_END
    n_pad = _round_up(n, PROJ_TN)
    pad_b = jnp.zeros((d_model, n_pad - n), w_in_l.dtype)
    return jnp.concatenate([g_m, head, pad_a, tail, pad_b], axis=1).astype(BF16)


def _compress_kernel(x_ref, pe_ref, w1_ref, w2_ref, o_ref, acc_ref, pacc_ref, *, nk, nb):
    kk = pl.program_id(1)

    @pl.when(kk == 0)
    def _():
        acc_ref[...] = jnp.zeros_like(acc_ref)
        pacc_ref[...] = jnp.zeros_like(pacc_ref)

    w1 = w1_ref[...]
    acc_ref[...] += _dot(x_ref[0].astype(BF16), w1)
    pacc_ref[...] += _dot(pe_ref[...].astype(BF16), w1)

    @pl.when(kk == nk - 1)
    def _():
        h = CMP_PRE_WIDTH
        pe_bias = pacc_ref[0:1, 0:h] + pacc_ref[1:2, h:2 * h]
        pre = acc_ref[0:nb, 0:h] + acc_ref[1:nb + 1, h:2 * h] + pe_bias
        act = jax.nn.gelu(pre)
        o_ref[0] = _dot(act.astype(BF16), w2_ref[...]).astype(BF16)


def _compress_weights(cmp_pe_l, cmp_w1_l, cmp_w2_l):
    halves = CMP_BLOCK // CMP_STRIDE
    w1h = cmp_w1_l.reshape(2, halves, CMP_STRIDE, HEAD_DIM, CMP_HIDDEN)
    eye_k = jnp.eye(2, dtype=F32)
    eye_g = jnp.eye(NSA_GROUPS, dtype=F32)
    w1big = jnp.einsum('khpdf,kK,gG->pKGdhkgf', w1h, eye_k, eye_g).reshape(CHUNK_WIDTH, halves * CMP_PRE_WIDTH)
    w2big = jnp.einsum('kfd,kK,gG->kgfGKd', cmp_w2_l, eye_k, eye_g).reshape(CMP_PRE_WIDTH, NSA_GROUPS * LANES)
    pe = cmp_pe_l.reshape(2, halves, CMP_STRIDE, HEAD_DIM).transpose(1, 2, 0, 3)
    pe = jnp.broadcast_to(pe[:, :, :, None, :], (halves, CMP_STRIDE, 2, NSA_GROUPS, HEAD_DIM))
    pe_rows = jnp.pad(pe.reshape(halves, CHUNK_WIDTH), ((0, 8 - halves), (0, 0)))
    return pe_rows, w1big.astype(BF16), w2big.astype(BF16)


def _compress(chunks, n_rows, weights):
    pe_rows, w1big, w2big = weights
    b = chunks.shape[0]
    n_chunks = -(-n_rows // CMP_STRIDE)
    nb = _round_up(n_chunks - 1, LANES)
    ncp = nb + 8
    if chunks.shape[1] < ncp:
        chunks = _pad_rows(chunks, 1, ncp)
    kstep = 2048
    nk = CHUNK_WIDTH // kstep
    nw = w1big.shape[1]
    return pl.pallas_call(
        functools.partial(_compress_kernel, nk=nk, nb=nb),
        grid=(b, nk),
        in_specs=[pl.BlockSpec((1, ncp, kstep), lambda i, k: (i, 0, k)),
                  pl.BlockSpec((8, kstep), lambda i, k: (0, k)),
                  pl.BlockSpec((kstep, nw), lambda i, k: (k, 0)),
                  pl.BlockSpec(w2big.shape, lambda i, k: (0, 0))],
        out_specs=pl.BlockSpec((1, nb, NSA_GROUPS * LANES), lambda i, k: (i, 0, 0)),
        out_shape=jax.ShapeDtypeStruct((b, nb, NSA_GROUPS * LANES), BF16),
        scratch_shapes=[pltpu.VMEM((ncp, nw), F32), pltpu.VMEM((8, nw), F32)],
        compiler_params=_cparams(("arbitrary", "arbitrary")),
        name="nsa_compress",
    )(chunks, pe_rows, w1big, w2big)


def _t5_bucket(d):
    n = jnp.maximum(d, 0)
    cnt = jnp.zeros_like(n)
    for th in _T5_THRESH:
        cnt = cnt + jnp.where(n >= th, 1, 0)
    return jnp.where(n < _MAX_EXACT, n, _MAX_EXACT + cnt)


def _bias_tiles(d, rel_ref, g):
    bucket = _t5_bucket(d)
    outs = [jnp.zeros(d.shape, F32) for _ in range(NSA_REP)]
    for j in range(REL_BUCKETS):
        hit = bucket == j
        for r in range(NSA_REP):
            outs[r] = jnp.where(hit, rel_ref[j, NSA_REP * g + r], outs[r])
    return tuple(outs)


def _softmax_step(sr, mask, kv, rs, m_ref, l_ref, acc_ref):
    m_old = m_ref[rs, :]
    m_new = jnp.maximum(m_old, jnp.max(sr, axis=1, keepdims=True))
    alpha = jnp.exp(m_old - m_new)
    p = jnp.exp(sr - m_new)
    if mask is not None:
        p = jnp.where(mask, p, 0.0)
    l_ref[rs, :] = alpha * l_ref[rs, :] + jnp.sum(p, axis=1, keepdims=True)
    acc_ref[rs, :] = alpha * acc_ref[rs, :] + _dot(p.astype(BF16), kv)
    m_ref[rs, :] = m_new


def _nsa_kernel(rel_ref, q_ref, gate_ref, cmp_ref, sel_ref, win_ref, ov_ref, o_ref,
                tab_ref, s_ref, acc_ref, m_ref, l_ref, *, tq, q_pos0, win_pos0, nb, nsbp, n_sb, ls, lw):
    g = pl.program_id(0)
    b = pl.program_id(1)
    qt = pl.program_id(2)
    t0 = q_pos0 + qt * tq
    rows = lax.broadcasted_iota(jnp.int32, (tq, LANES), 0)
    cols = lax.broadcasted_iota(jnp.int32, (tq, LANES), 1)
    tpos = t0 + rows
    far_idx = N_BIAS_TILES - 1

    @pl.when((b == 0) & (qt == 0))
    def _():
        def body(di, c):
            outs = _bias_tiles(di * KEY_TILE + rows - cols, rel_ref, g)
            for r in range(NSA_REP):
                tab_ref[di, r] = outs[r]
            return c
        lax.fori_loop(0, N_BIAS_TILES, body, 0)

    q = q_ref[0].reshape(NSA_REP * tq, LANES)
    row_slices = [pl.ds(r * tq, tq) for r in range(NSA_REP)]

    for ct in range(nb // LANES):
        kc = cmp_ref[0, ct * LANES:(ct + 1) * LANES, :]
        s = _dot_nt(q, kc)
        d = tpos - (CMP_BLOCK - 1) - CMP_STRIDE * (ct * LANES + cols)
        d_min = t0 - (CMP_BLOCK - 1) - CMP_STRIDE * (ct * LANES + LANES - 1)

        def far_fn():
            return tuple(jnp.full((tq, LANES), rel_ref[REL_BUCKETS - 1, NSA_REP * g + r], F32)
                         for r in range(NSA_REP))

        def near_fn(d=d):
            return _bias_tiles(d, rel_ref, g)

        bias = lax.cond(d_min >= _FAR_DIST, far_fn, near_fn)
        valid = d >= 0
        for r in range(NSA_REP):
            s_ref[row_slices[r], ct * LANES:(ct + 1) * LANES] = jnp.where(
                valid, s[r * tq:(r + 1) * tq] + bias[r], NEG_INF)

    s = s_ref[...]
    valid = s > 0.5 * NEG_INF
    p = jnp.where(valid, jnp.exp(s - jnp.max(s, axis=1, keepdims=True)), 0.0)
    pc = p / jnp.maximum(jnp.sum(p, axis=1, keepdims=True), 1e-30)
    o_cmp = _dot(pc.astype(BF16), cmp_ref[0])

    psum = pc[0:tq] + pc[tq:2 * tq] + pc[2 * tq:3 * tq] + pc[3 * tq:4 * tq]
    p_hi = psum.astype(BF16)
    p_lo = (psum - p_hi.astype(F32)).astype(BF16)
    imp = _dot(p_hi, ov_ref[...]) + _dot(p_lo, ov_ref[...])
    blk = lax.broadcasted_iota(jnp.int32, (tq, nsbp), 1)
    blk_f = blk.astype(F32)
    qb = jnp.right_shift(t0 + lax.broadcasted_iota(jnp.int32, (tq, nsbp), 0), 6)
    forced = (blk == 0) | (blk == qb) | (blk == qb - 1)
    score = jnp.where(forced, FORCED_SCORE, jnp.where(blk <= qb, imp, -1.0))
    score = jnp.where(blk < n_sb, score, -2.0)
    sel = jnp.zeros((tq, nsbp), F32)
    for _ in range(min(SEL_TOPN, n_sb)):
        mx = jnp.max(score, axis=1, keepdims=True)
        first = jnp.min(jnp.where(score == mx, blk_f, 1e9), axis=1, keepdims=True)
        pick = blk_f == first
        sel = jnp.where(pick, 1.0, sel)
        score = jnp.where(pick, -3.0, score)
    sel_pen = ((sel - 1.0) * 1e30).astype(BF16)

    def reset():
        m_ref[...] = jnp.full(m_ref.shape, NEG_INF, F32)
        l_ref[...] = jnp.zeros(l_ref.shape, F32)
        acc_ref[...] = jnp.zeros(acc_ref.shape, F32)

    reset()
    blk_row = lax.broadcasted_iota(jnp.int32, (nsbp, LANES), 0)
    blk_of_col = jnp.right_shift(lax.broadcasted_iota(jnp.int32, (nsbp, LANES), 1), 6)

    def sel_body(kt, c):
        k0 = pl.multiple_of(kt * KEY_TILE, KEY_TILE)
        kv = sel_ref[0, 0, pl.ds(k0, KEY_TILE), :]
        s = _dot_nt(q, kv)
        expand = jnp.where(blk_row == (KEY_TILE // SEL_BLOCK) * kt + blk_of_col, 1.0, 0.0).astype(BF16)
        pen = _dot(sel_pen, expand)
        pen = jnp.where(k0 + cols <= tpos, pen, NEG_INF)
        di = jnp.minimum(jnp.right_shift(t0 - k0, 7), far_idx)
        for r in range(NSA_REP):
            sr = s[r * tq:(r + 1) * tq] + tab_ref[di, r] + pen
            _softmax_step(sr, None, kv, row_slices[r], m_ref, l_ref, acc_ref)
        return c

    n_kt = jnp.minimum(jnp.right_shift(t0 + tq - 1, 7) + 1, ls // KEY_TILE)
    lax.fori_loop(0, n_kt, sel_body, 0)
    o_sel = acc_ref[...] / jnp.maximum(l_ref[...], 1e-30)

    reset()

    def win_body(wt, c):
        k0 = pl.multiple_of(wt * KEY_TILE, KEY_TILE)
        kv = win_ref[0, 0, pl.ds(k0, KEY_TILE), :]
        s = _dot_nt(q, kv)
        delta = t0 - win_pos0 - k0
        d = delta + rows - cols
        mask = (d >= 0) & (d < WINDOW)
        di = jnp.minimum(jnp.right_shift(delta, 7), far_idx)
        for r in range(NSA_REP):
            sr = jnp.where(mask, s[r * tq:(r + 1) * tq] + tab_ref[di, r], NEG_INF)
            _softmax_step(sr, mask, kv, row_slices[r], m_ref, l_ref, acc_ref)
        return c

    wt_lo = jnp.maximum(jnp.right_shift(t0 - (WINDOW - 1) - win_pos0, 7), 0)
    wt_hi = jnp.minimum(jnp.right_shift(t0 + tq - 1 - win_pos0, 7), lw // KEY_TILE - 1)
    lax.fori_loop(wt_lo, wt_hi + 1, win_body, 0)
    o_win = acc_ref[...] / jnp.maximum(l_ref[...], 1e-30)

    gt = jax.nn.sigmoid(gate_ref[0, 0])
    heads = []
    for r in range(NSA_REP):
        rs = slice(r * tq, (r + 1) * tq)
        heads.append(gt[:, r:r + 1] * o_cmp[rs] + gt[:, NSA_REP + r:NSA_REP + r + 1] * o_sel[rs]
                     + gt[:, 2 * NSA_REP + r:2 * NSA_REP + r + 1] * o_win[rs])
    low = cols < HEAD_DIM
    for pair in range(NSA_REP // 2):
        left = pltpu.roll(heads[2 * pair], HEAD_DIM, 1)
        o_ref[0, :, pair * LANES:(pair + 1) * LANES] = jnp.where(low, left, heads[2 * pair + 1]).astype(BF16)


def _overlap_matrix(nb, nsbp):
    cs = np.arange(nb)[:, None] * CMP_STRIDE
    ss = np.arange(nsbp)[None, :] * SEL_BLOCK
    ov = np.minimum(cs + CMP_BLOCK, ss + SEL_BLOCK) - np.maximum(cs, ss)
    return jnp.asarray(np.maximum(ov, 0).astype(np.float32) / CMP_BLOCK, dtype=BF16)


def _nsa_attention(rel_bias, q, gates, cmpkv, selkv, winkv, *, tq, q_pos0, win_pos0, n_sel_rows):
    b, _, tp, _ = q.shape
    nb = cmpkv.shape[1]
    ls = selkv.shape[2]
    lw = winkv.shape[2]
    n_sb = -(-n_sel_rows // SEL_BLOCK)
    nsbp = _round_up(n_sb, LANES)
    assert q_pos0 % KEY_TILE == 0 and win_pos0 % KEY_TILE == 0 and tq <= KEY_TILE and n_sb >= SEL_TOPN
    assert ls % KEY_TILE == 0 and lw % KEY_TILE == 0 and tp % tq == 0
    ov = _overlap_matrix(nb, nsbp)
    kern = functools.partial(_nsa_kernel, tq=tq, q_pos0=q_pos0, win_pos0=win_pos0, nb=nb, nsbp=nsbp,
                             n_sb=n_sb, ls=ls, lw=lw)
    return pl.pallas_call(
        kern,
        grid=(NSA_GROUPS, b, tp // tq),
        in_specs=[pl.BlockSpec(memory_space=pltpu.SMEM),
                  pl.BlockSpec((1, NSA_REP, tq, LANES), lambda g, i, t: (i, g, t, 0)),
                  pl.BlockSpec((1, 1, tq, LANES), lambda g, i, t: (i, g, t, 0)),
                  pl.BlockSpec((1, nb, LANES), lambda g, i, t: (i, 0, g)),
                  pl.BlockSpec((1, 1, ls, LANES), lambda g, i, t: (i, g, 0, 0)),
                  pl.BlockSpec((1, 1, lw, LANES), lambda g, i, t: (i, g, 0, 0)),
                  pl.BlockSpec((nb, nsbp), lambda g, i, t: (0, 0))],
        out_specs=pl.BlockSpec((1, tq, NSA_REP * HEAD_DIM), lambda g, i, t: (i, t, g)),
        out_shape=jax.ShapeDtypeStruct((b, tp, NSA_WIDTH), BF16),
        scratch_shapes=[pltpu.VMEM((N_BIAS_TILES, NSA_REP, tq, LANES), F32),
                        pltpu.VMEM((NSA_REP * tq, nb), F32),
                        pltpu.VMEM((NSA_REP * tq, LANES), F32),
                        pltpu.VMEM((NSA_REP * tq, 1), F32),
                        pltpu.VMEM((NSA_REP * tq, 1), F32)],
        compiler_params=_cparams(("arbitrary", "arbitrary", "arbitrary")),
        name="nsa_attention",
    )(rel_bias, q, gates, cmpkv, selkv, winkv, ov)


def _sb_kernel(q_ref, k_ref, v_ref, u_ref, o_ref, acc_ref, car_ref, *, tq, q_pos0, lp):
    qt = pl.program_id(2)
    t0 = q_pos0 + qt * tq
    q = q_ref[0, 0]
    rows = lax.broadcasted_iota(jnp.int32, (tq, LANES), 0)
    cols = lax.broadcasted_iota(jnp.int32, (tq, LANES), 1)
    tpos = t0 + rows
    scale = SB_HEAD_DIM ** -0.5
    acc_ref[...] = jnp.zeros(acc_ref.shape, F32)
    car_ref[...] = jnp.zeros(car_ref.shape, F32)

    def cond(c):
        kt, go = c
        return (kt >= 0) & go

    def body(c):
        kt, _ = c
        k0 = pl.multiple_of(kt * KEY_TILE, KEY_TILE)
        k = k_ref[0, 0, pl.ds(k0, KEY_TILE), :]
        v = v_ref[0, 0, pl.ds(k0, KEY_TILE), :]
        z = _dot_nt(q, k) * scale
        causal = k0 + cols < tpos
        softplus = jnp.maximum(z, 0.0) + jnp.log(1.0 + jnp.exp(-jnp.abs(z)))
        log_stay = jnp.where(causal, -softplus, 0.0)
        hi = log_stay.astype(BF16)
        lo = (log_stay - hi.astype(F32)).astype(BF16)
        sums = _dot(jnp.concatenate([hi, lo], axis=1), u_ref[...])
        carry = car_ref[...]
        a = jnp.where(causal, jnp.exp((z - softplus) + sums[:, :LANES] + carry), 0.0)
        acc_ref[...] += _dot(a.astype(BF16), v)
        carry = carry + sums[:, LANES:]
        car_ref[...] = carry
        return kt - 1, jnp.max(carry) > EXP_UNDERFLOW

    kt_hi = jnp.minimum(jnp.right_shift(t0 + tq - 2, 7), lp // KEY_TILE - 1)
    lax.while_loop(cond, body, (kt_hi, True))
    o_ref[0] = acc_ref[...].astype(BF16)


def _suffix_matrix():
    j = np.arange(2 * KEY_TILE)[:, None] % KEY_TILE
    s = np.arange(2 * KEY_TILE)[None, :]
    u = np.where(s < KEY_TILE, j > s, True)
    return jnp.asarray(u.astype(np.float32), dtype=BF16)


def _sb_attention(q, k, v, *, tq, q_pos0):
    b, _, tp, _ = q.shape
    lp = k.shape[2]
    assert lp % KEY_TILE == 0 and tp % tq == 0 and q_pos0 % KEY_TILE == 0 and tq >= 2
    return pl.pallas_call(
        functools.partial(_sb_kernel, tq=tq, q_pos0=q_pos0, lp=lp),
        grid=(b, SB_HEADS, tp // tq),
        in_specs=[pl.BlockSpec((1, 1, tq, SB_HEAD_DIM), lambda i, h, t: (i, h, t, 0)),
                  pl.BlockSpec((1, 1, lp, SB_HEAD_DIM), lambda i, h, t: (i, h, 0, 0)),
                  pl.BlockSpec((1, 1, lp, SB_HEAD_DIM), lambda i, h, t: (i, h, 0, 0)),
                  pl.BlockSpec((2 * KEY_TILE, 2 * KEY_TILE), lambda i, h, t: (0, 0))],
        out_specs=pl.BlockSpec((1, tq, SB_HEAD_DIM), lambda i, h, t: (i, t, h)),
        out_shape=jax.ShapeDtypeStruct((b, tp, SB_WIDTH), BF16),
        scratch_shapes=[pltpu.VMEM((tq, SB_HEAD_DIM), F32), pltpu.VMEM((tq, LANES), F32)],
        compiler_params=_cparams(("arbitrary", "arbitrary", "arbitrary")),
        name="sb_attention",
    )(q, k, v, _suffix_matrix())


def _finish_kernel(oa_ref, ob_ref, gma_ref, gmb_ref, x_ref, wa_ref, wb_ref, wo_ref, g1_ref, b1_ref,
                   wr_ref, br_ref, h_ref, comb_ref, *, alpha, n_experts):
    ua = _dot(oa_ref[...], wa_ref[...])
    ub = _dot(ob_ref[...], wb_ref[...])
    mixed = jax.nn.sigmoid(gma_ref[...]) * ua + jax.nn.sigmoid(gmb_ref[...]) * ub
    mo = _dot(mixed.astype(BF16), wo_ref[...])
    h = _layer_norm(alpha * x_ref[...] + mo, g1_ref[...], b1_ref[...])
    h_ref[...] = h
    logits = _dot(h.astype(BF16), wr_ref[...]) + br_ref[...]
    lane = lax.broadcasted_iota(jnp.int32, logits.shape, 1)
    lane_f = lane.astype(F32)
    sc = jnp.where(lane < n_experts, logits, NEG_INF)
    vals, picks = [], []
    for _ in range(TOP_K):
        mx = jnp.max(sc, axis=1, keepdims=True)
        first = jnp.min(jnp.where(sc == mx, lane_f, 1e9), axis=1, keepdims=True)
        pick = lane_f == first
        vals.append(mx)
        picks.append(pick)
        sc = jnp.where(pick, 2.0 * NEG_INF, sc)
    es = [jnp.exp(v - vals[0]) for v in vals]
    den = es[0] + es[1] + es[2] + es[3]
    comb = jnp.zeros(logits.shape, F32)
    for k in range(TOP_K):
        comb = jnp.where(picks[k], es[k] / den, comb)
    comb_ref[...] = comb


def _finish(o_a, o_b, y, x, weights, *, tm, d_model, alpha, n_experts):
    wa, wb, wo, g1, b1, wr, br = weights
    m = x.shape[0]
    full = lambda a: pl.BlockSpec(a.shape, lambda i: (0, 0), pipeline_mode=pl.Buffered(1))
    return pl.pallas_call(
        functools.partial(_finish_kernel, alpha=alpha, n_experts=n_experts),
        grid=(m // tm,),
        in_specs=[pl.BlockSpec((tm, NSA_WIDTH), lambda i: (i, 0)),
                  pl.BlockSpec((tm, SB_WIDTH), lambda i: (i, 0)),
                  pl.BlockSpec((tm, d_model), lambda i: (i, 0)),
                  pl.BlockSpec((tm, d_model), lambda i: (i, 1)),
                  pl.BlockSpec((tm, d_model), lambda i: (i, 0)),
                  full(wa), full(wb), full(wo), full(g1), full(b1), full(wr), full(br)],
        out_specs=[pl.BlockSpec((tm, d_model), lambda i: (i, 0)),
                   pl.BlockSpec((tm, LANES), lambda i: (i, 0))],
        out_shape=[jax.ShapeDtypeStruct((m, d_model), F32), jax.ShapeDtypeStruct((m, LANES), F32)],
        compiler_params=_cparams(("arbitrary",)),
        name="out_proj_ln_route",
    )(o_a, o_b, y, y, x, wa, wb, wo, g1, b1, wr, br)


def _moe_kernel(h_ref, comb_ref, wg_ref, wu_ref, bg_ref, bu_ref, wd_ref, bd_ref, g2_ref, b2_ref, o_ref,
                hb_ref, acc_ref, *, alpha, n_experts):
    e = pl.program_id(1)

    @pl.when(e == 0)
    def _():
        hb_ref[...] = h_ref[...].astype(BF16)
        acc_ref[...] = jnp.zeros_like(acc_ref)

    hb = hb_ref[...]
    gate = jnp.minimum(_dot(hb, wg_ref[0]) + bg_ref[0], SWIGLU_LIMIT)
    up = jnp.clip(_dot(hb, wu_ref[0]) + bu_ref[0], -SWIGLU_LIMIT, SWIGLU_LIMIT)
    act = (up + 1.0) * gate * jax.nn.sigmoid(SWIGLU_ALPHA * gate)
    y = _dot(act.astype(BF16), wd_ref[0]) + bd_ref[0]
    lane = lax.broadcasted_iota(jnp.int32, comb_ref.shape, 1)
    c = jnp.sum(jnp.where(lane == e, comb_ref[...], 0.0), axis=1, keepdims=True)
    acc_ref[...] += c * y

    @pl.when(e == n_experts - 1)
    def _():
        o_ref[...] = _layer_norm(alpha * h_ref[...] + acc_ref[...], g2_ref[...], b2_ref[...])


def _moe(h, comb, weights, *, tm, alpha):
    wg, wu, bg, bu, wd, bd, g2, b2 = weights
    m, d_model = h.shape
    n_experts, _, d_ff = wg.shape
    return pl.pallas_call(
        functools.partial(_moe_kernel, alpha=alpha, n_experts=n_experts),
        grid=(m // tm, n_experts),
        in_specs=[pl.BlockSpec((tm, d_model), lambda i, e: (i, 0)),
                  pl.BlockSpec((tm, LANES), lambda i, e: (i, 0)),
                  pl.BlockSpec((1, d_model, d_ff), lambda i, e: (e, 0, 0)),
                  pl.BlockSpec((1, d_model, d_ff), lambda i, e: (e, 0, 0)),
                  pl.BlockSpec((1, 1, d_ff), lambda i, e: (e, 0, 0)),
                  pl.BlockSpec((1, 1, d_ff), lambda i, e: (e, 0, 0)),
                  pl.BlockSpec((1, d_ff, d_model), lambda i, e: (e, 0, 0)),
                  pl.BlockSpec((1, 1, d_model), lambda i, e: (e, 0, 0)),
                  pl.BlockSpec((1, d_model), lambda i, e: (0, 0)),
                  pl.BlockSpec((1, d_model), lambda i, e: (0, 0))],
        out_specs=pl.BlockSpec((tm, d_model), lambda i, e: (i, 0)),
        out_shape=jax.ShapeDtypeStruct((m, d_model), F32),
        scratch_shapes=[pltpu.VMEM((tm, d_model), BF16), pltpu.VMEM((tm, d_model), F32)],
        compiler_params=_cparams(("arbitrary", "arbitrary")),
        name="moe_ln",
    )(h, comb, wg, wu, bg, bu, wd, bd, g2, b2)


def _gather_kernel(pt_ref, *refs, n_out, n_steps, mode):
    page_refs = refs[:PAGES_PER_STEP]
    new_ref = refs[PAGES_PER_STEP]
    rest = refs[PAGES_PER_STEP + 1:]
    perm_ref = rest[0] if mode == "nsa" else None
    out_refs = rest[-n_out:]
    j = pl.program_id(1)

    def emit(x, r0, n):
        if mode == "raw":
            out_refs[0][0, r0:r0 + n, :] = x
        elif mode == "nsa":
            y = _dot(x.astype(BF16), perm_ref[...])
            for g in range(NSA_GROUPS):
                out_refs[0][0, g, r0:r0 + n, :] = y[:, g * LANES:(g + 1) * LANES].astype(BF16)
        else:
            for h in range(SB_HEADS):
                out_refs[0][0, h, r0:r0 + n, :] = x[:, h * SB_HEAD_DIM:(h + 1) * SB_HEAD_DIM].astype(BF16)
                out_refs[1][0, h, r0:r0 + n, :] = x[:, SB_WIDTH + h * SB_HEAD_DIM:
                                                    SB_WIDTH + (h + 1) * SB_HEAD_DIM].astype(BF16)

    @pl.when(j < n_steps)
    def _():
        for i in range(PAGES_PER_STEP):
            emit(page_refs[i][0], i * PAGE_SIZE, PAGE_SIZE)

    @pl.when(j == n_steps)
    def _():
        for o in out_refs:
            o[...] = jnp.zeros(o.shape, o.dtype)
        emit(new_ref[0], 0, 8)


def _nsa_perm_matrix():
    p = np.zeros((NSA_KV_WIDTH, NSA_KV_WIDTH), np.float32)
    for kv in range(2):
        for g in range(NSA_GROUPS):
            for d in range(HEAD_DIM):
                p[kv * NSA_GROUPS * HEAD_DIM + g * HEAD_DIM + d, g * LANES + kv * HEAD_DIM + d] = 1.0
    return jnp.asarray(p, dtype=BF16)


def _gather_pages(cache_l, page_table, new_rows, mode):
    b, n_pages = page_table.shape
    w = cache_l.shape[-1]
    assert n_pages % PAGES_PER_STEP == 0
    n_steps = n_pages // PAGES_PER_STEP
    rows_step = PAGES_PER_STEP * PAGE_SIZE
    lp = (n_steps + 1) * rows_step
    new_blk = jnp.pad(new_rows[:, None, :], ((0, 0), (0, 7), (0, 0)))

    def page_spec(i):
        return pl.BlockSpec((1, PAGE_SIZE, w),
                            lambda s, j, pt: (pt[s, jnp.minimum(j * PAGES_PER_STEP + i, n_pages - 1)], 0, 0))

    in_specs = [page_spec(i) for i in range(PAGES_PER_STEP)]
    in_specs.append(pl.BlockSpec((1, 8, w), lambda s, j, pt: (s, 0, 0)))
    args = [cache_l] * PAGES_PER_STEP + [new_blk]
    if mode == "raw":
        out_shape = [jax.ShapeDtypeStruct((b, lp, w), F32)]
        out_specs = [pl.BlockSpec((1, rows_step, w), lambda s, j, pt: (s, j, 0))]
    elif mode == "nsa":
        perm = _nsa_perm_matrix()
        in_specs.append(pl.BlockSpec(perm.shape, lambda s, j, pt: (0, 0)))
        args.append(perm)
        out_shape = [jax.ShapeDtypeStruct((b, NSA_GROUPS, lp, LANES), BF16)]
        out_specs = [pl.BlockSpec((1, NSA_GROUPS, rows_step, LANES), lambda s, j, pt: (s, 0, j, 0))]
    else:
        out_shape = [jax.ShapeDtypeStruct((b, SB_HEADS, lp, SB_HEAD_DIM), BF16)] * 2
        out_specs = [pl.BlockSpec((1, SB_HEADS, rows_step, SB_HEAD_DIM), lambda s, j, pt: (s, 0, j, 0))] * 2
    outs = pl.pallas_call(
        functools.partial(_gather_kernel, n_out=len(out_shape), n_steps=n_steps, mode=mode),
        grid_spec=pltpu.PrefetchScalarGridSpec(
            num_scalar_prefetch=1, grid=(b, n_steps + 1), in_specs=in_specs, out_specs=out_specs),
        out_shape=out_shape,
        compiler_params=_cparams(("arbitrary", "arbitrary")),
        name="gather_" + mode,
    )(page_table, *args)
    return outs


def _pack_nsa_kv(kv):
    b, l, _ = kv.shape
    return kv.reshape(b, l, 2, NSA_GROUPS, HEAD_DIM).transpose(0, 3, 1, 2, 4).reshape(
        b, NSA_GROUPS, l, LANES).astype(BF16)


def _nsa_queries(y, off):
    b, t, _ = y.shape
    q = y[:, :, off + _OFF_QA:off + _OFF_QA + NSA_WIDTH].reshape(b, t, NSA_HEADS, HEAD_DIM) * (HEAD_DIM ** -0.5)
    q = jnp.pad(q.transpose(0, 2, 1, 3), ((0, 0), (0, 0), (0, 0), (0, LANES - HEAD_DIM)))
    return q.astype(BF16)


def _nsa_gates(y, off):
    b, t, _ = y.shape
    g = y[:, :, off + _OFF_GA:off + _OFF_GA + 3 * NSA_HEADS].reshape(b, t, 3, NSA_GROUPS, NSA_REP)
    g = g.transpose(0, 3, 1, 2, 4).reshape(b, NSA_GROUPS, t, 3 * NSA_REP)
    return jnp.pad(g, ((0, 0), (0, 0), (0, 0), (0, LANES - 3 * NSA_REP)))


def _pad_rows(x, axis, n):
    pad = [(0, 0)] * x.ndim
    pad[axis] = (0, n - x.shape[axis])
    return jnp.pad(x, pad)


def kernel(x_prompt, x_sample, cache_cmp_kv, cache_sel_kv, cache_sb_kv, state_win_kv, page_table, rel_bias,
           w_in, cmp_pe, cmp_w1, cmp_w2, w_up_nsa, w_up_sb, w_out, ln1_g, ln1_b,
           w_router, b_router, w_gate_up, b_gate_up, w_down, b_down, ln2_g, ln2_b):
    depth, d_model = w_in.shape[0], w_in.shape[1]
    bp, seq, _ = x_prompt.shape
    bs, dec_seq, _ = x_sample.shape
    assert bp == 1 and dec_seq == 1
    n_pages = page_table.shape[1]
    past = n_pages * PAGE_SIZE
    n_buf = state_win_kv.shape[2]
    n_experts = w_router.shape[2]
    alpha = (2 * depth) ** 0.25
    off = 2 * d_model
    tq_s = 8

    hp = x_prompt.reshape(seq, d_model)
    hs = x_sample.reshape(bs, d_model)
    outs = [[] for _ in range(8)]
    for l in range(depth):
        w_proj = _proj_weight(w_in[l], d_model)
        cmp_w = _compress_weights(cmp_pe[l], cmp_w1[l], cmp_w2[l])
        fin_w = (w_up_nsa[l].astype(BF16), w_up_sb[l].astype(BF16), w_out[l].astype(BF16),
                 ln1_g[l][None], ln1_b[l][None],
                 jnp.pad(w_router[l], ((0, 0), (0, LANES - n_experts))).astype(BF16),
                 jnp.pad(b_router[l], (0, LANES - n_experts))[None])
        moe_w = (w_gate_up[l][:, :, 0::2].astype(BF16), w_gate_up[l][:, :, 1::2].astype(BF16),
                 b_gate_up[l][:, None, 0::2], b_gate_up[l][:, None, 1::2],
                 w_down[l].astype(BF16), b_down[l][:, None, :], ln2_g[l][None], ln2_b[l][None])

        tm = min(512, seq)
        y = _matmul(hp, w_proj, tm, PROJ_TN)
        yb = y[None]
        kv_c = y[:, off + _OFF_KVC:off + _OFF_KVC + NSA_KV_WIDTH]
        kv_s = y[:, off + _OFF_KVS:off + _OFF_KVS + NSA_KV_WIDTH]
        kv_w = y[:, off + _OFF_KVW:off + _OFF_KVW + NSA_KV_WIDTH]
        kv_b = y[:, off + _OFF_KVB:off + _OFF_KVB + 2 * SB_WIDTH]
        n_chunks = -(-seq // CMP_STRIDE)
        ncp = _round_up(n_chunks - 1, LANES) + 8
        chunks = _pad_rows(_pad_rows(kv_c, 0, n_chunks * CMP_STRIDE).reshape(1, n_chunks, CHUNK_WIDTH), 1, ncp)
        cmpkv = _compress(chunks, seq, cmp_w)
        lpad = _round_up(seq, KEY_TILE)
        o_a = _nsa_attention(rel_bias, _nsa_queries(yb, off), _nsa_gates(yb, off), cmpkv,
                             _pad_rows(_pack_nsa_kv(kv_s[None]), 2, lpad), _pad_rows(_pack_nsa_kv(kv_w[None]), 2, lpad),
                             tq=KEY_TILE, q_pos0=0, win_pos0=0, n_sel_rows=seq)
        q_b = y[:, off + _OFF_QB:off + _OFF_QB + SB_WIDTH].reshape(1, seq, SB_HEADS, SB_HEAD_DIM)
        kvb = kv_b.reshape(1, seq, 2, SB_HEADS, SB_HEAD_DIM).astype(BF16)
        o_b = _sb_attention(q_b.transpose(0, 2, 1, 3).astype(BF16),
                            _pad_rows(kvb[:, :, 0].transpose(0, 2, 1, 3), 2, lpad),
                            _pad_rows(kvb[:, :, 1].transpose(0, 2, 1, 3), 2, lpad), tq=KEY_TILE, q_pos0=0)
        tm_f = min(256, seq)
        h1, comb = _finish(o_a[0], o_b[0], y, hp, fin_w, tm=tm_f, d_model=d_model, alpha=alpha,
                           n_experts=n_experts)
        hp = _moe(h1, comb, moe_w, tm=min(512, seq), alpha=alpha)
        n_win = min(WINDOW, seq)
        outs[0].append(kv_c.reshape(1, seq, 2, NSA_GROUPS, HEAD_DIM))
        outs[1].append(kv_s.reshape(1, seq, 2, NSA_GROUPS, HEAD_DIM))
        outs[2].append(kv_b.reshape(1, seq, 2, SB_HEADS, SB_HEAD_DIM))
        outs[3].append(kv_w[seq - n_win:].reshape(1, n_win, 2, NSA_GROUPS, HEAD_DIM))

        ys = _matmul(hs, w_proj, bs, PROJ_TN)
        ysb = ys[:, None, :]
        kv_c = ys[:, off + _OFF_KVC:off + _OFF_KVC + NSA_KV_WIDTH]
        kv_s = ys[:, off + _OFF_KVS:off + _OFF_KVS + NSA_KV_WIDTH]
        kv_w = ys[:, off + _OFF_KVW:off + _OFF_KVW + NSA_KV_WIDTH]
        kv_b = ys[:, off + _OFF_KVB:off + _OFF_KVB + 2 * SB_WIDTH]
        n_pool = cache_cmp_kv.shape[1]
        (cmp_rows,) = _gather_pages(cache_cmp_kv[l].reshape(n_pool, PAGE_SIZE, NSA_KV_WIDTH), page_table, kv_c, "raw")
        (sel_rows,) = _gather_pages(cache_sel_kv[l].reshape(n_pool, PAGE_SIZE, NSA_KV_WIDTH), page_table, kv_s, "nsa")
        sb_k, sb_v = _gather_pages(cache_sb_kv[l].reshape(n_pool, PAGE_SIZE, 2 * SB_WIDTH), page_table, kv_b, "sb")
        lp = cmp_rows.shape[1]
        cmpkv = _compress(cmp_rows.reshape(bs, lp // CMP_STRIDE, CHUNK_WIDTH), past + 1, cmp_w)
        win_all = jnp.concatenate([state_win_kv[l].reshape(bs, n_buf, NSA_KV_WIDTH), kv_w[:, None, :]], axis=1)
        winkv = _pad_rows(_pack_nsa_kv(win_all), 2, _round_up(n_buf + 1, KEY_TILE))
        o_a = _nsa_attention(rel_bias, _pad_rows(_nsa_queries(ysb, off), 2, tq_s), _pad_rows(_nsa_gates(ysb, off), 2, tq_s),
                             cmpkv, sel_rows, winkv, tq=tq_s, q_pos0=past, win_pos0=past - n_buf,
                             n_sel_rows=past + 1)
        q_b = ys[:, off + _OFF_QB:off + _OFF_QB + SB_WIDTH].reshape(bs, 1, SB_HEADS, SB_HEAD_DIM)
        o_b = _sb_attention(_pad_rows(q_b.transpose(0, 2, 1, 3).astype(BF16), 2, tq_s), sb_k, sb_v,
                            tq=tq_s, q_pos0=past)
        h1, comb = _finish(o_a[:, 0], o_b[:, 0], ys, hs, fin_w, tm=bs, d_model=d_model, alpha=alpha,
                           n_experts=n_experts)
        hs = _moe(h1, comb, moe_w, tm=bs, alpha=alpha)
        outs[4].append(kv_c.reshape(bs, 1, 2, NSA_GROUPS, HEAD_DIM))
        outs[5].append(kv_s.reshape(bs, 1, 2, NSA_GROUPS, HEAD_DIM))
        outs[6].append(kv_b.reshape(bs, 1, 2, SB_HEADS, SB_HEAD_DIM))
        outs[7].append(win_all[:, 1:].reshape(bs, n_buf, 2, NSA_GROUPS, HEAD_DIM))

    return (hp.reshape(bp, seq, d_model), hs.reshape(bs, dec_seq, d_model)) + tuple(jnp.stack(o) for o in outs)
```

```python
import functools
import math

import numpy as np
import jax
import jax.numpy as jnp
from jax import lax
from jax.experimental import pallas as pl
from jax.experimental.pallas import tpu as pltpu

F32 = jnp.float32
BF16 = jnp.bfloat16

NSA_HEADS = 16
NSA_GROUPS = 4
NSA_REP = NSA_HEADS // NSA_GROUPS
HEAD_DIM = 64
CMP_BLOCK = 32
CMP_STRIDE = 16
CMP_HIDDEN = 64
SEL_BLOCK = 64
SEL_TOPN = 16
WINDOW = 512
SB_HEADS = 8
SB_HEAD_DIM = 128
REL_BUCKETS = 32
REL_MAX_DIST = 4096
TOP_K = 4
SWIGLU_LIMIT = 7.0
SWIGLU_ALPHA = 1.702
LN_EPS = 1e-5
NEG_INF = -1e30
FORCED_SCORE = 1e4
PAGE_SIZE = 128

LANES = 128
KEY_TILE = 128
SEL_TILE = 512
SEL_TILE_BLOCKS = SEL_TILE // SEL_BLOCK
NSA_WIDTH = NSA_HEADS * HEAD_DIM
SB_WIDTH = SB_HEADS * SB_HEAD_DIM
NSA_KV_WIDTH = 2 * NSA_GROUPS * HEAD_DIM
NSA_PACK_WIDTH = NSA_GROUPS * LANES
CHUNK_WIDTH = CMP_STRIDE * NSA_KV_WIDTH
CMP_PRE_WIDTH = 2 * NSA_GROUPS * CMP_HIDDEN
VMEM_LIMIT = 56 * 1024 * 1024
PAGES_PER_STEP = 8
SB_PAGES_PER_STEP = 4
EXP_UNDERFLOW = -104.0

_MAX_EXACT = REL_BUCKETS // 2
_T5_THRESH = tuple(int(math.ceil(_MAX_EXACT * 2.0 ** (j / 2.0) - 1e-9)) for j in range(1, REL_BUCKETS - _MAX_EXACT))
_FAR_DIST = _T5_THRESH[-1]
N_BIAS_TILES = -(-(_FAR_DIST + KEY_TILE) // KEY_TILE) + 1

_OFF_QA = 0
_OFF_KVC = _OFF_QA + NSA_WIDTH
_OFF_KVS = _OFF_KVC + NSA_KV_WIDTH
_OFF_KVW = _OFF_KVS + NSA_KV_WIDTH
_OFF_GA = _OFF_KVW + NSA_KV_WIDTH
_OFF_QB = _OFF_GA + LANES
_OFF_KVB = _OFF_QB + SB_WIDTH
_OFF_END = _OFF_KVB + 2 * SB_WIDTH
PROJ_TN = 768


def _cparams(sem):
    return pltpu.CompilerParams(dimension_semantics=sem, vmem_limit_bytes=VMEM_LIMIT)


def _round_up(x, m):
    return -(-x // m) * m


def _dot(a, b):
    return jnp.dot(a, b, preferred_element_type=F32)


def _dot_nt(a, b):
    return lax.dot_general(a, b, (((1,), (1,)), ((), ())), preferred_element_type=F32)


def _layer_norm(x, g, b):
    mu = jnp.mean(x, axis=-1, keepdims=True)
    xc = x - mu
    var = jnp.mean(xc * xc, axis=-1, keepdims=True)
    return xc * lax.rsqrt(var + LN_EPS) * g + b


def _pad_rows(x, axis, n):
    pad = [(0, 0)] * x.ndim
    pad[axis] = (0, n - x.shape[axis])
    return jnp.pad(x, pad)


def _matmul_kernel(x_ref, w_ref, o_ref, ob_ref, xb_ref):
    @pl.when(pl.program_id(1) == 0)
    def _():
        xb_ref[...] = x_ref[...].astype(BF16)

    acc = _dot(xb_ref[...], w_ref[...])
    o_ref[...] = acc
    ob_ref[...] = acc.astype(BF16)


def _in_proj(x, w, tm, tn):
    m, k = x.shape
    n = w.shape[1]
    return pl.pallas_call(
        _matmul_kernel,
        grid=(m // tm, n // tn),
        in_specs=[pl.BlockSpec((tm, k), lambda i, j: (i, 0)),
                  pl.BlockSpec((k, tn), lambda i, j: (0, j))],
        out_specs=[pl.BlockSpec((tm, tn), lambda i, j: (i, j)),
                   pl.BlockSpec((tm, tn), lambda i, j: (i, j))],
        out_shape=[jax.ShapeDtypeStruct((m, n), F32), jax.ShapeDtypeStruct((m, n), BF16)],
        scratch_shapes=[pltpu.VMEM((tm, k), BF16)],
        compiler_params=_cparams(("arbitrary", "arbitrary")),
        name="in_proj",
    )(x, w)


def _proj_weight(w_in_l, d_model):
    o = np.cumsum((0, NSA_WIDTH, NSA_KV_WIDTH, NSA_KV_WIDTH, NSA_KV_WIDTH, 3 * NSA_HEADS,
                   SB_WIDTH, 2 * SB_WIDTH, 2 * d_model))
    g_m = w_in_l[:, o[7]:o[8]]
    head = w_in_l[:, o[0]:o[5]]
    tail = w_in_l[:, o[5]:o[7]]
    pad_a = jnp.zeros((d_model, LANES - 3 * NSA_HEADS), w_in_l.dtype)
    n = 2 * d_model + _OFF_END
    n_pad = _round_up(n, PROJ_TN)
    pad_b = jnp.zeros((d_model, n_pad - n), w_in_l.dtype)
    return jnp.concatenate([g_m, head, pad_a, tail, pad_b], axis=1).astype(BF16)


def _pack_kernel(qa_ref, qb_ref, ks_ref, kw_ref, pq_ref, pk_ref, pv_ref, pkv_ref,
                 q_ref, ka_ref, vo_ref, win_ref, *, tm, pos0, pos_stride, transpose_vo):
    q_ref[:, 0:NSA_HEADS * LANES // 2] = _dot(qa_ref[...], pq_ref[...]).astype(BF16)
    q_ref[:, NSA_HEADS * LANES // 2:] = _dot(qb_ref[...], pq_ref[...]).astype(BF16)
    lane = lax.broadcasted_iota(jnp.int32, (tm, NSA_PACK_WIDTH), 1) % LANES
    pos = pos0 + pos_stride * (pl.program_id(0) * tm + lax.broadcasted_iota(jnp.int32, (tm, NSA_PACK_WIDTH), 0))
    local_blk = jnp.right_shift(pos % SEL_TILE, 6)
    tag = (lane >= HEAD_DIM) & (lane - HEAD_DIM == local_blk)
    ka_ref[...] = jnp.where(tag, 1.0, _dot(ks_ref[...], pk_ref[...])).astype(BF16)
    vo = jnp.where(lane >= HEAD_DIM, 1.0, _dot(ks_ref[...], pv_ref[...]))
    vo_ref[...] = (vo.T if transpose_vo else vo).astype(BF16)
    win_ref[...] = _dot(kw_ref[...], pkv_ref[...]).astype(BF16)


def _pack_matrices():
    half = NSA_WIDTH // 2
    pq = np.zeros((half, half * 2), np.float32)
    for h in range(NSA_HEADS // 2):
        for d in range(HEAD_DIM):
            pq[h * HEAD_DIM + d, h * LANES + d] = HEAD_DIM ** -0.5
    pk = np.zeros((NSA_KV_WIDTH, NSA_PACK_WIDTH), np.float32)
    pv = np.zeros((NSA_KV_WIDTH, NSA_PACK_WIDTH), np.float32)
    pkv = np.zeros((NSA_KV_WIDTH, NSA_PACK_WIDTH), np.float32)
    for g in range(NSA_GROUPS):
        for d in range(HEAD_DIM):
            k_col = g * HEAD_DIM + d
            v_col = NSA_GROUPS * HEAD_DIM + g * HEAD_DIM + d
            pk[k_col, g * LANES + d] = 1.0
            pv[v_col, g * LANES + d] = 1.0
            pkv[k_col, g * LANES + d] = 1.0
            pkv[v_col, g * LANES + HEAD_DIM + d] = 1.0
    return tuple(jnp.asarray(a, dtype=BF16) for a in (pq, pk, pv, pkv))


def _pack(yb, off, tm, pos0, pos_stride, transpose_vo):
    m = yb.shape[0]
    assert off % NSA_KV_WIDTH == 0 and SEL_TILE_BLOCKS <= LANES - HEAD_DIM
    c0 = off // NSA_KV_WIDTH
    mats = _pack_matrices()
    blk = lambda c: pl.BlockSpec((tm, NSA_KV_WIDTH), lambda i, c=c: (i, c))
    full = lambda a: pl.BlockSpec(a.shape, lambda i: (0, 0))
    rows_spec = lambda w: pl.BlockSpec((tm, w), lambda i: (i, 0))
    rows_shape = lambda w: jax.ShapeDtypeStruct((m, w), BF16)
    vo_spec = pl.BlockSpec((NSA_PACK_WIDTH, tm), lambda i: (0, i)) if transpose_vo else rows_spec(NSA_PACK_WIDTH)
    vo_shape = jax.ShapeDtypeStruct((NSA_PACK_WIDTH, m), BF16) if transpose_vo else rows_shape(NSA_PACK_WIDTH)
    return pl.pallas_call(
        functools.partial(_pack_kernel, tm=tm, pos0=pos0, pos_stride=pos_stride, transpose_vo=transpose_vo),
        grid=(m // tm,),
        in_specs=[blk(c0), blk(c0 + 1), blk(c0 + _OFF_KVS // NSA_KV_WIDTH), blk(c0 + _OFF_KVW // NSA_KV_WIDTH)]
        + [full(a) for a in mats],
        out_specs=[rows_spec(NSA_HEADS * LANES), rows_spec(NSA_PACK_WIDTH), vo_spec, rows_spec(NSA_PACK_WIDTH)],
        out_shape=[rows_shape(NSA_HEADS * LANES), rows_shape(NSA_PACK_WIDTH), vo_shape, rows_shape(NSA_PACK_WIDTH)],
        compiler_params=_cparams(("arbitrary",)),
        name="nsa_pack",
    )(yb, yb, yb, yb, *mats)


def _compress_kernel(x_ref, pe_ref, w1_ref, w2_ref, o_ref, acc_ref, pacc_ref, *, nk, nb):
    kk = pl.program_id(1)

    @pl.when(kk == 0)
    def _():
        acc_ref[...] = jnp.zeros_like(acc_ref)
        pacc_ref[...] = jnp.zeros_like(pacc_ref)

    w1 = w1_ref[...]
    acc_ref[...] += _dot(x_ref[0], w1)
    pacc_ref[...] += _dot(pe_ref[...].astype(BF16), w1)

    @pl.when(kk == nk - 1)
    def _():
        h = CMP_PRE_WIDTH
        pe_bias = pacc_ref[0:1, 0:h] + pacc_ref[1:2, h:2 * h]
        pre = acc_ref[0:nb, 0:h] + acc_ref[1:nb + 1, h:2 * h] + pe_bias
        act = jax.nn.gelu(pre)
        o_ref[0] = _dot(act.astype(BF16), w2_ref[...]).astype(BF16)


def _compress_weights(cmp_pe_l, cmp_w1_l, cmp_w2_l):
    halves = CMP_BLOCK // CMP_STRIDE
    w1h = cmp_w1_l.reshape(2, halves, CMP_STRIDE, HEAD_DIM, CMP_HIDDEN)
    eye_k = jnp.eye(2, dtype=F32)
    eye_g = jnp.eye(NSA_GROUPS, dtype=F32)
    w1big = jnp.einsum('khpdf,kK,gG->pKGdhkgf', w1h, eye_k, eye_g).reshape(CHUNK_WIDTH, halves * CMP_PRE_WIDTH)
    w2big = jnp.einsum('kfd,kK,gG->kgfGKd', cmp_w2_l, eye_k, eye_g).reshape(CMP_PRE_WIDTH, NSA_PACK_WIDTH)
    pe = cmp_pe_l.reshape(2, halves, CMP_STRIDE, HEAD_DIM).transpose(1, 2, 0, 3)
    pe = jnp.broadcast_to(pe[:, :, :, None, :], (halves, CMP_STRIDE, 2, NSA_GROUPS, HEAD_DIM))
    pe_rows = jnp.pad(pe.reshape(halves, CHUNK_WIDTH), ((0, 8 - halves), (0, 0)))
    return pe_rows, w1big.astype(BF16), w2big.astype(BF16)


def _compress(chunks, n_rows, weights):
    pe_rows, w1big, w2big = weights
    b = chunks.shape[0]
    n_chunks = -(-n_rows // CMP_STRIDE)
    nb = _round_up(n_chunks - 1, LANES)
    ncp = nb + 16
    if chunks.shape[1] < ncp:
        chunks = _pad_rows(chunks, 1, ncp)
    kstep = 2048
    nk = CHUNK_WIDTH // kstep
    nw = w1big.shape[1]
    return pl.pallas_call(
        functools.partial(_compress_kernel, nk=nk, nb=nb),
        grid=(b, nk),
        in_specs=[pl.BlockSpec((1, ncp, kstep), lambda i, k: (i, 0, k)),
                  pl.BlockSpec((8, kstep), lambda i, k: (0, k)),
                  pl.BlockSpec((kstep, nw), lambda i, k: (k, 0)),
                  pl.BlockSpec(w2big.shape, lambda i, k: (0, 0))],
        out_specs=pl.BlockSpec((1, nb, NSA_PACK_WIDTH), lambda i, k: (i, 0, 0)),
        out_shape=jax.ShapeDtypeStruct((b, nb, NSA_PACK_WIDTH), BF16),
        scratch_shapes=[pltpu.VMEM((ncp, nw), F32), pltpu.VMEM((8, nw), F32)],
        compiler_params=_cparams(("arbitrary", "arbitrary")),
        name="nsa_compress",
    )(chunks, pe_rows, w1big, w2big)


def _t5_bucket(d):
    n = jnp.maximum(d, 0)
    cnt = jnp.zeros_like(n)
    for th in _T5_THRESH:
        cnt = cnt + jnp.where(n >= th, 1, 0)
    return jnp.where(n < _MAX_EXACT, n, _MAX_EXACT + cnt)


def _bias_tiles(d, rel_ref, g):
    bucket = _t5_bucket(d)
    outs = [jnp.zeros(d.shape, F32) for _ in range(NSA_REP)]
    for j in range(REL_BUCKETS):
        hit = bucket == j
        for r in range(NSA_REP):
            outs[r] = jnp.where(hit, rel_ref[j, NSA_REP * g + r], outs[r])
    return tuple(outs)


def _softmax_step(sr, mask, kv, rs, m_ref, l_ref, acc_ref):
    m_old = m_ref[rs, :]
    m_new = jnp.maximum(m_old, jnp.max(sr, axis=1, keepdims=True))
    alpha = jnp.exp(m_old - m_new)
    p = jnp.where(mask, jnp.exp(sr - m_new), 0.0)
    l_ref[rs, :] = alpha * l_ref[rs, :] + jnp.sum(p, axis=1, keepdims=True)
    acc_ref[rs, :] = alpha * acc_ref[rs, :] + _dot(p.astype(BF16), kv)
    m_ref[rs, :] = m_new


def _sel_head_step(st, shift, cs, mt_ref, acct_ref, pt_ref):
    m_old = mt_ref[:, cs]
    m_new = jnp.maximum(m_old, jnp.max(st, axis=0, keepdims=True) + shift)
    acct_ref[:, cs] = jnp.exp(m_old - m_new) * acct_ref[:, cs]
    pt_ref[:, cs] = jnp.exp(st - (m_new - shift)).astype(BF16)
    mt_ref[:, cs] = m_new


def _nsa_kernel(rel_ref, q_ref, gate_ref, cmp_ref, ka_ref, vot_ref, win_ref, ov_ref, o_ref,
                tab_ref, tabt_ref, s_ref, pent_ref, qat_ref, st_ref, pt_ref, acct_ref, mt_ref,
                acc_ref, m_ref, l_ref, *,
                tq, n_real, q_pos0, win_pos0, nb, nsbp, n_sb, ls, lw, skip_empty):
    g = pl.program_id(0)
    b = pl.program_id(1)
    qt = pl.program_id(2)
    t0 = q_pos0 + qt * tq
    rows = lax.broadcasted_iota(jnp.int32, (tq, LANES), 0)
    cols = lax.broadcasted_iota(jnp.int32, (tq, LANES), 1)
    tpos = t0 + rows
    far_idx = N_BIAS_TILES - 1

    @pl.when((b == 0) & (qt == 0))
    def _():
        def body(di, c):
            outs = _bias_tiles(di * KEY_TILE + rows - cols, rel_ref, g)
            outs_t = _bias_tiles(di * KEY_TILE + cols - rows, rel_ref, g)
            for r in range(NSA_REP):
                tab_ref[di, r] = outs[r]
                tabt_ref[di, r] = outs_t[r]
            return c
        lax.fori_loop(0, N_BIAS_TILES, body, 0)

    qs = [q_ref[0, :, r * LANES:(r + 1) * LANES] for r in range(NSA_REP)]
    q = jnp.concatenate(qs, axis=0)
    row_slices = [pl.ds(r * tq, tq) for r in range(NSA_REP)]

    for ct in range(nb // LANES):
        kc = cmp_ref[0, ct * LANES:(ct + 1) * LANES, :]
        s = _dot_nt(q, kc)
        d = tpos - (CMP_BLOCK - 1) - CMP_STRIDE * (ct * LANES + cols)
        d_min = t0 - (CMP_BLOCK - 1) - CMP_STRIDE * (ct * LANES + LANES - 1)

        def far_fn():
            return tuple(jnp.full((tq, LANES), rel_ref[REL_BUCKETS - 1, NSA_REP * g + r], F32)
                         for r in range(NSA_REP))

        def near_fn(d=d):
            return _bias_tiles(d, rel_ref, g)

        bias = lax.cond(d_min >= _FAR_DIST, far_fn, near_fn)
        valid = d >= 0
        for r in range(NSA_REP):
            s_ref[row_slices[r], ct * LANES:(ct + 1) * LANES] = jnp.where(
                valid, s[r * tq:(r + 1) * tq] + bias[r], NEG_INF)

    s = s_ref[:, 0:nb]
    valid = s > 0.5 * NEG_INF
    p = jnp.where(valid, jnp.exp(s - jnp.max(s, axis=1, keepdims=True)), 0.0)
    pc = p / jnp.maximum(jnp.sum(p, axis=1, keepdims=True), 1e-30)
    o_cmp = _dot(pc.astype(BF16), cmp_ref[0])

    psum = pc[0:tq] + pc[tq:2 * tq] + pc[2 * tq:3 * tq] + pc[3 * tq:4 * tq]
    p_hi = psum.astype(BF16)
    p_lo = (psum - p_hi.astype(F32)).astype(BF16)
    imp = _dot(p_hi, ov_ref[...]) + _dot(p_lo, ov_ref[...])
    blk = lax.broadcasted_iota(jnp.int32, (tq, nsbp), 1)
    blk_f = blk.astype(F32)
    sel_rows = lax.broadcasted_iota(jnp.int32, (tq, nsbp), 0)
    qb = jnp.right_shift(t0 + sel_rows, 6)
    forced = (blk == 0) | (blk == qb) | (blk == qb - 1)
    score = jnp.where(forced, FORCED_SCORE, jnp.where(blk <= qb, imp, -1.0))
    score = jnp.where(blk < n_sb, score, -2.0)
    score = score.T
    blk_t = lax.broadcasted_iota(jnp.int32, (nsbp, tq), 0).astype(F32)
    sel_t = jnp.zeros((nsbp, tq), F32)
    for _ in range(min(SEL_TOPN, n_sb)):
        mx = jnp.max(score, axis=0, keepdims=True)
        first = jnp.min(jnp.where(score == mx, blk_t, 1e9), axis=0, keepdims=True)
        pick = blk_t == first
        sel_t = jnp.where(pick, 1.0, sel_t)
        score = jnp.where(pick, -3.0, score)
    if n_real < tq:
        real_q = lax.broadcasted_iota(jnp.int32, (nsbp, tq), 1) < n_real
        sel_t = jnp.where(real_q, sel_t, 0.0)
    pent_ref[...] = (sel_t - 1.0) * 1e30

    mt_ref[...] = jnp.full(mt_ref.shape, NEG_INF, F32)
    acct_ref[...] = jnp.zeros(acct_ref.shape, F32)
    qat_ref[...] = q.astype(F32).T
    col_slices = [slice(r * tq, (r + 1) * tq) for r in range(NSA_REP)]
    key_i = lax.broadcasted_iota(jnp.int32, (SEL_TILE, tq), 0)
    qry_i = lax.broadcasted_iota(jnp.int32, (SEL_TILE, tq), 1)
    real_pen = lax.broadcasted_iota(jnp.int32, (SEL_TILE_BLOCKS, tq), 1) < n_real

    def sel_body(kt, c):
        k0 = pl.multiple_of(kt * SEL_TILE, SEL_TILE)
        pen = pent_ref[pl.ds(pl.multiple_of(kt * SEL_TILE_BLOCKS, SEL_TILE_BLOCKS), SEL_TILE_BLOCKS), :]

        def compute():
            for r in range(NSA_REP):
                qat_ref[HEAD_DIM:HEAD_DIM + SEL_TILE_BLOCKS, col_slices[r]] = pen
            st_ref[...] = _dot(ka_ref[0, pl.ds(k0, SEL_TILE), :], qat_ref[...].astype(BF16))
            far = t0 - (k0 + SEL_TILE - 1) >= _FAR_DIST

            @pl.when(far)
            def _():
                for r in range(NSA_REP):
                    _sel_head_step(st_ref[:, col_slices[r]], rel_ref[REL_BUCKETS - 1, NSA_REP * g + r],
                                   col_slices[r], mt_ref, acct_ref, pt_ref)

            @pl.when(jnp.logical_not(far))
            def _():
                di0 = jnp.right_shift(t0 - k0, 7)
                causal = k0 + key_i <= t0 + qry_i
                for r in range(NSA_REP):
                    bias = jnp.concatenate(
                        [tabt_ref[jnp.clip(di0 - c4, 0, far_idx), r] for c4 in range(SEL_TILE // KEY_TILE)], axis=0)
                    st = jnp.where(causal, st_ref[:, col_slices[r]] + bias, NEG_INF)
                    _sel_head_step(st, 0.0, col_slices[r], mt_ref, acct_ref, pt_ref)

            acct_ref[...] += _dot(vot_ref[0, :, pl.ds(k0, SEL_TILE)], pt_ref[...])

        if skip_empty:
            pl.when(jnp.max(jnp.where(real_pen, pen, NEG_INF)) > -1.0)(compute)
        else:
            compute()
        return c

    n_kt = jnp.minimum(jnp.right_shift(t0 + tq - 1, 9) + 1, ls // SEL_TILE)
    lax.fori_loop(0, n_kt, sel_body, 0)
    acct = acct_ref[...]
    o_sel_t = acct[0:HEAD_DIM] / jnp.maximum(acct[HEAD_DIM:], 1e-30)
    o_sel = jnp.concatenate([jnp.zeros_like(o_sel_t), o_sel_t], axis=0).T

    m_ref[...] = jnp.full(m_ref.shape, NEG_INF, F32)
    l_ref[...] = jnp.zeros(l_ref.shape, F32)
    acc_ref[...] = jnp.zeros(acc_ref.shape, F32)

    def win_body(wt, c):
        k0 = pl.multiple_of(wt * KEY_TILE, KEY_TILE)
        kv = win_ref[0, pl.ds(k0, KEY_TILE), :]
        s = _dot_nt(q, kv)
        delta = t0 - win_pos0 - k0
        d = delta + rows - cols
        mask = (d >= 0) & (d < WINDOW)
        di = jnp.minimum(jnp.right_shift(delta, 7), far_idx)
        for r in range(NSA_REP):
            sr = jnp.where(mask, s[r * tq:(r + 1) * tq] + tab_ref[di, r], NEG_INF)
            _softmax_step(sr, mask, kv, row_slices[r], m_ref, l_ref, acc_ref)
        return c

    wt_lo = jnp.maximum(jnp.right_shift(t0 - (WINDOW - 1) - win_pos0, 7), 0)
    wt_hi = jnp.minimum(jnp.right_shift(t0 + tq - 1 - win_pos0, 7), lw // KEY_TILE - 1)
    lax.fori_loop(wt_lo, wt_hi + 1, win_body, 0)
    o_win = acc_ref[...] / jnp.maximum(l_ref[...], 1e-30)

    gt = jax.nn.sigmoid(gate_ref[0, 0])
    heads = []
    for r in range(NSA_REP):
        rs = slice(r * tq, (r + 1) * tq)
        heads.append(gt[:, r:r + 1] * o_cmp[rs] + gt[:, NSA_REP + r:NSA_REP + r + 1] * o_sel[rs]
                     + gt[:, 2 * NSA_REP + r:2 * NSA_REP + r + 1] * o_win[rs])
    low = cols < HEAD_DIM
    for pair in range(NSA_REP // 2):
        left = pltpu.roll(heads[2 * pair], HEAD_DIM, 1)
        o_ref[0, :, pair * LANES:(pair + 1) * LANES] = jnp.where(low, left, heads[2 * pair + 1]).astype(BF16)


def _overlap_matrix(nb, nsbp):
    cs = np.arange(nb)[:, None] * CMP_STRIDE
    ss = np.arange(nsbp)[None, :] * SEL_BLOCK
    ov = np.minimum(cs + CMP_BLOCK, ss + SEL_BLOCK) - np.maximum(cs, ss)
    return jnp.asarray(np.maximum(ov, 0).astype(np.float32) / CMP_BLOCK, dtype=BF16)


def _nsa_attention(rel_bias, q, gates, cmpkv, ka, vot, winkv, *, n_real, q_pos0, win_pos0, n_sel_rows,
                   skip_empty):
    b, tp, _ = q.shape
    nb = cmpkv.shape[1]
    ls = ka.shape[1]
    lw = winkv.shape[1]
    tq = LANES
    n_sb = -(-n_sel_rows // SEL_BLOCK)
    nsbp = _round_up(n_sb, LANES)
    assert q_pos0 % SEL_TILE == 0 and win_pos0 % KEY_TILE == 0 and n_sb >= SEL_TOPN
    assert ls % SEL_TILE == 0 and lw % KEY_TILE == 0 and tp % tq == 0 and vot.shape[2] == ls
    ov = _overlap_matrix(nb, nsbp)
    kern = functools.partial(_nsa_kernel, tq=tq, n_real=n_real, q_pos0=q_pos0, win_pos0=win_pos0, nb=nb,
                             nsbp=nsbp, n_sb=n_sb, ls=ls, lw=lw, skip_empty=skip_empty)
    return pl.pallas_call(
        kern,
        grid=(NSA_GROUPS, b, tp // tq),
        in_specs=[pl.BlockSpec(memory_space=pltpu.SMEM),
                  pl.BlockSpec((1, tq, NSA_REP * LANES), lambda g, i, t: (i, t, g)),
                  pl.BlockSpec((1, 1, tq, LANES), lambda g, i, t: (i, g, t, 0)),
                  pl.BlockSpec((1, nb, LANES), lambda g, i, t: (i, 0, g)),
                  pl.BlockSpec((1, ls, LANES), lambda g, i, t: (i, 0, g)),
                  pl.BlockSpec((1, LANES, ls), lambda g, i, t: (i, g, 0)),
                  pl.BlockSpec((1, lw, LANES), lambda g, i, t: (i, 0, g)),
                  pl.BlockSpec((nb, nsbp), lambda g, i, t: (0, 0))],
        out_specs=pl.BlockSpec((1, tq, NSA_REP * HEAD_DIM), lambda g, i, t: (i, t, g)),
        out_shape=jax.ShapeDtypeStruct((b, tp, NSA_WIDTH), BF16),
        scratch_shapes=[pltpu.VMEM((N_BIAS_TILES, NSA_REP, tq, LANES), F32),
                        pltpu.VMEM((N_BIAS_TILES, NSA_REP, KEY_TILE, tq), F32),
                        pltpu.VMEM((NSA_REP * tq, nb), F32),
                        pltpu.VMEM((nsbp, tq), F32),
                        pltpu.VMEM((LANES, NSA_REP * tq), F32),
                        pltpu.VMEM((SEL_TILE, NSA_REP * tq), F32),
                        pltpu.VMEM((SEL_TILE, NSA_REP * tq), BF16),
                        pltpu.VMEM((LANES, NSA_REP * tq), F32),
                        pltpu.VMEM((1, NSA_REP * tq), F32),
                        pltpu.VMEM((NSA_REP * tq, LANES), F32),
                        pltpu.VMEM((NSA_REP * tq, 1), F32),
                        pltpu.VMEM((NSA_REP * tq, 1), F32)],
        compiler_params=_cparams(("arbitrary", "arbitrary", "arbitrary")),
        name="nsa_attention",
    )(rel_bias, q, gates, cmpkv, ka, vot, winkv, ov)


def _sb_tile(q, k, v, u, causal, carry):
    z = _dot_nt(q, k) * (SB_HEAD_DIM ** -0.5)
    softplus = jnp.maximum(z, 0.0) + jnp.log(1.0 + jnp.exp(-jnp.abs(z)))
    log_stay = -softplus if causal is None else jnp.where(causal, -softplus, 0.0)
    hi = log_stay.astype(BF16)
    lo = (log_stay - hi.astype(F32)).astype(BF16)
    sums = _dot(jnp.concatenate([hi, lo], axis=1), u)
    a = jnp.exp((z - softplus) + sums[:, :LANES] + carry)
    if causal is not None:
        a = jnp.where(causal, a, 0.0)
    return _dot(a.astype(BF16), v), carry + sums[:, LANES:]


def _sb_kernel(q_ref, k_ref, v_ref, u_ref, o_ref, acc_ref, car_ref, *, tq, q_pos0, lp):
    qt = pl.program_id(2)
    t0 = q_pos0 + qt * tq
    q = q_ref[0]
    rows = lax.broadcasted_iota(jnp.int32, (tq, LANES), 0)
    cols = lax.broadcasted_iota(jnp.int32, (tq, LANES), 1)
    tpos = t0 + rows
    acc_ref[...] = jnp.zeros(acc_ref.shape, F32)
    car_ref[...] = jnp.zeros(car_ref.shape, F32)

    def cond(c):
        kt, go = c
        return (kt >= 0) & go

    def body(c):
        kt, _ = c
        k0 = pl.multiple_of(kt * KEY_TILE, KEY_TILE)
        pv, carry = _sb_tile(q, k_ref[0, pl.ds(k0, KEY_TILE), :], v_ref[0, pl.ds(k0, KEY_TILE), :], u_ref[...],
                             k0 + cols < tpos, car_ref[...])
        acc_ref[...] += pv
        car_ref[...] = carry
        return kt - 1, jnp.max(carry) > EXP_UNDERFLOW

    kt_hi = jnp.minimum(jnp.right_shift(t0 + tq - 2, 7), lp // KEY_TILE - 1)
    lax.while_loop(cond, body, (kt_hi, True))
    o_ref[0] = acc_ref[...].astype(BF16)


def _suffix_matrix():
    j = np.arange(2 * KEY_TILE)[:, None] % KEY_TILE
    s = np.arange(2 * KEY_TILE)[None, :]
    u = np.where(s < KEY_TILE, j > s, True)
    return jnp.asarray(u.astype(np.float32), dtype=BF16)


def _sb_attention(yb, off, *, tq):
    _, t, _ = yb.shape
    assert t % KEY_TILE == 0 and t % tq == 0 and tq >= 2
    cq = (off + _OFF_QB) // SB_HEAD_DIM
    ck = (off + _OFF_KVB) // SB_HEAD_DIM
    cv = ck + SB_HEADS
    return pl.pallas_call(
        functools.partial(_sb_kernel, tq=tq, q_pos0=0, lp=t),
        grid=(1, SB_HEADS, t // tq),
        in_specs=[pl.BlockSpec((1, tq, SB_HEAD_DIM), lambda i, h, s: (i, s, cq + h)),
                  pl.BlockSpec((1, t, SB_HEAD_DIM), lambda i, h, s: (i, 0, ck + h)),
                  pl.BlockSpec((1, t, SB_HEAD_DIM), lambda i, h, s: (i, 0, cv + h)),
                  pl.BlockSpec((2 * KEY_TILE, 2 * KEY_TILE), lambda i, h, s: (0, 0))],
        out_specs=pl.BlockSpec((1, tq, SB_HEAD_DIM), lambda i, h, s: (i, s, h)),
        out_shape=jax.ShapeDtypeStruct((1, t, SB_WIDTH), BF16),
        scratch_shapes=[pltpu.VMEM((tq, SB_HEAD_DIM), F32), pltpu.VMEM((tq, LANES), F32)],
        compiler_params=_cparams(("arbitrary", "arbitrary", "arbitrary")),
        name="sb_attention",
    )(yb, yb, yb, _suffix_matrix())


def _sb_decode_kernel(pt_ref, q_ref, u_ref, *refs, n_steps):
    page_refs = refs[:SB_PAGES_PER_STEP]
    o_ref, acc_ref, car_ref, done_ref = refs[SB_PAGES_PER_STEP:]
    jj = pl.program_id(1)

    @pl.when(jj == 0)
    def _():
        acc_ref[...] = jnp.zeros(acc_ref.shape, F32)
        car_ref[...] = jnp.zeros(car_ref.shape, F32)
        done_ref[0] = 0

    for i in range(SB_PAGES_PER_STEP):
        @pl.when(done_ref[0] == 0)
        def _(i=i):
            mx = jnp.float32(2.0 * NEG_INF)
            for h in range(SB_HEADS):
                k = page_refs[i][0, pl.ds(h, PAGE_SIZE, stride=2 * SB_HEADS), :].astype(BF16)
                v = page_refs[i][0, pl.ds(SB_HEADS + h, PAGE_SIZE, stride=2 * SB_HEADS), :].astype(BF16)
                pv, carry = _sb_tile(q_ref[0, h], k, v, u_ref[...], None, car_ref[h])
                acc_ref[h] += pv
                car_ref[h] = carry
                mx = jnp.maximum(mx, jnp.max(carry[0:1, :]))
            done_ref[0] = jnp.where(mx > EXP_UNDERFLOW, 0, 1)

    @pl.when(jj == n_steps - 1)
    def _():
        for h in range(SB_HEADS):
            o_ref[0, h:h + 1, :] = acc_ref[h][0:1, :]


def _sb_decode(cache_l, page_table, q):
    b, n_pages = page_table.shape
    assert n_pages % SB_PAGES_PER_STEP == 0
    n_steps = n_pages // SB_PAGES_PER_STEP
    rows = cache_l.shape[1]
    qr = q.shape[2]

    def page_spec(i):
        return pl.BlockSpec((1, rows, SB_HEAD_DIM),
                            lambda s, j, pt: (pt[s, n_pages - 1 - (j * SB_PAGES_PER_STEP + i)], 0, 0))

    in_specs = [pl.BlockSpec((1, SB_HEADS, qr, SB_HEAD_DIM), lambda s, j, pt: (s, 0, 0, 0)),
                pl.BlockSpec((2 * KEY_TILE, 2 * KEY_TILE), lambda s, j, pt: (0, 0))]
    in_specs += [page_spec(i) for i in range(SB_PAGES_PER_STEP)]
    return pl.pallas_call(
        functools.partial(_sb_decode_kernel, n_steps=n_steps),
        grid_spec=pltpu.PrefetchScalarGridSpec(
            num_scalar_prefetch=1, grid=(b, n_steps), in_specs=in_specs,
            out_specs=pl.BlockSpec((1, SB_HEADS, SB_HEAD_DIM), lambda s, j, pt: (s, 0, 0)),
            scratch_shapes=[pltpu.VMEM((SB_HEADS, qr, SB_HEAD_DIM), F32), pltpu.VMEM((SB_HEADS, qr, LANES), F32),
                            pltpu.SMEM((1,), jnp.int32)]),
        out_shape=jax.ShapeDtypeStruct((b, SB_HEADS, SB_HEAD_DIM), F32),
        compiler_params=_cparams(("arbitrary", "arbitrary")),
        name="sb_decode",
    )(page_table, q, _suffix_matrix(), *([cache_l] * SB_PAGES_PER_STEP))


def _finish_kernel(oa_ref, ob_ref, gma_ref, gmb_ref, x_ref, wa_ref, wb_ref, wo_ref, g1_ref, b1_ref,
                   wr_ref, br_ref, h_ref, comb_ref, *, alpha, n_experts):
    ua = _dot(oa_ref[...], wa_ref[...])
    ub = _dot(ob_ref[...], wb_ref[...])
    mixed = jax.nn.sigmoid(gma_ref[...]) * ua + jax.nn.sigmoid(gmb_ref[...]) * ub
    mo = _dot(mixed.astype(BF16), wo_ref[...])
    h = _layer_norm(alpha * x_ref[...] + mo, g1_ref[...], b1_ref[...])
    h_ref[...] = h
    logits = _dot(h.astype(BF16), wr_ref[...]) + br_ref[...]
    lane = lax.broadcasted_iota(jnp.int32, logits.shape, 1)
    lane_f = lane.astype(F32)
    sc = jnp.where(lane < n_experts, logits, NEG_INF)
    vals, picks = [], []
    for _ in range(TOP_K):
        mx = jnp.max(sc, axis=1, keepdims=True)
        first = jnp.min(jnp.where(sc == mx, lane_f, 1e9), axis=1, keepdims=True)
        pick = lane_f == first
        vals.append(mx)
        picks.append(pick)
        sc = jnp.where(pick, 2.0 * NEG_INF, sc)
    es = [jnp.exp(v - vals[0]) for v in vals]
    den = es[0] + es[1] + es[2] + es[3]
    comb = jnp.zeros(logits.shape, F32)
    for k in range(TOP_K):
        comb = jnp.where(picks[k], es[k] / den, comb)
    comb_ref[...] = comb


def _finish(o_a, o_b, y, x, weights, *, tm, d_model, alpha, n_experts):
    wa, wb, wo, g1, b1, wr, br = weights
    m = x.shape[0]
    full = lambda a: pl.BlockSpec(a.shape, lambda i: (0, 0), pipeline_mode=pl.Buffered(1))
    return pl.pallas_call(
        functools.partial(_finish_kernel, alpha=alpha, n_experts=n_experts),
        grid=(m // tm,),
        in_specs=[pl.BlockSpec((tm, NSA_WIDTH), lambda i: (i, 0)),
                  pl.BlockSpec((tm, SB_WIDTH), lambda i: (i, 0)),
                  pl.BlockSpec((tm, d_model), lambda i: (i, 0)),
                  pl.BlockSpec((tm, d_model), lambda i: (i, 1)),
                  pl.BlockSpec((tm, d_model), lambda i: (i, 0)),
                  full(wa), full(wb), full(wo), full(g1), full(b1), full(wr), full(br)],
        out_specs=[pl.BlockSpec((tm, d_model), lambda i: (i, 0)),
                   pl.BlockSpec((tm, LANES), lambda i: (i, 0))],
        out_shape=[jax.ShapeDtypeStruct((m, d_model), F32), jax.ShapeDtypeStruct((m, LANES), F32)],
        compiler_params=_cparams(("arbitrary",)),
        name="out_proj_ln_route",
    )(o_a, o_b, y, y, x, wa, wb, wo, g1, b1, wr, br)


def _moe_kernel(h_ref, comb_ref, wg_ref, wu_ref, bg_ref, bu_ref, wd_ref, bd_ref, g2_ref, b2_ref, o_ref,
                hb_ref, acc_ref, *, alpha, n_experts):
    e = pl.program_id(1)

    @pl.when(e == 0)
    def _():
        hb_ref[...] = h_ref[...].astype(BF16)
        acc_ref[...] = jnp.zeros_like(acc_ref)

    hb = hb_ref[...]
    gate = jnp.minimum(_dot(hb, wg_ref[0]) + bg_ref[0], SWIGLU_LIMIT)
    up = jnp.clip(_dot(hb, wu_ref[0]) + bu_ref[0], -SWIGLU_LIMIT, SWIGLU_LIMIT)
    act = (up + 1.0) * gate * jax.nn.sigmoid(SWIGLU_ALPHA * gate)
    y = _dot(act.astype(BF16), wd_ref[0]) + bd_ref[0]
    lane = lax.broadcasted_iota(jnp.int32, comb_ref.shape, 1)
    c = jnp.sum(jnp.where(lane == e, comb_ref[...], 0.0), axis=1, keepdims=True)
    acc_ref[...] += c * y

    @pl.when(e == n_experts - 1)
    def _():
        o_ref[...] = _layer_norm(alpha * h_ref[...] + acc_ref[...], g2_ref[...], b2_ref[...])


def _moe(h, comb, weights, *, tm, alpha):
    wg, wu, bg, bu, wd, bd, g2, b2 = weights
    m, d_model = h.shape
    n_experts, _, d_ff = wg.shape
    return pl.pallas_call(
        functools.partial(_moe_kernel, alpha=alpha, n_experts=n_experts),
        grid=(m // tm, n_experts),
        in_specs=[pl.BlockSpec((tm, d_model), lambda i, e: (i, 0)),
                  pl.BlockSpec((tm, LANES), lambda i, e: (i, 0)),
                  pl.BlockSpec((1, d_model, d_ff), lambda i, e: (e, 0, 0)),
                  pl.BlockSpec((1, d_model, d_ff), lambda i, e: (e, 0, 0)),
                  pl.BlockSpec((1, 1, d_ff), lambda i, e: (e, 0, 0)),
                  pl.BlockSpec((1, 1, d_ff), lambda i, e: (e, 0, 0)),
                  pl.BlockSpec((1, d_ff, d_model), lambda i, e: (e, 0, 0)),
                  pl.BlockSpec((1, 1, d_model), lambda i, e: (e, 0, 0)),
                  pl.BlockSpec((1, d_model), lambda i, e: (0, 0)),
                  pl.BlockSpec((1, d_model), lambda i, e: (0, 0))],
        out_specs=pl.BlockSpec((tm, d_model), lambda i, e: (i, 0)),
        out_shape=jax.ShapeDtypeStruct((m, d_model), F32),
        scratch_shapes=[pltpu.VMEM((tm, d_model), BF16), pltpu.VMEM((tm, d_model), F32)],
        compiler_params=_cparams(("arbitrary", "arbitrary")),
        name="moe_ln",
    )(h, comb, wg, wu, bg, bu, wd, bd, g2, b2)


def _gather_cmp_kernel(pt_ref, *refs, n_steps):
    page_refs = refs[:PAGES_PER_STEP]
    new_ref, o_ref, rows_ref = refs[PAGES_PER_STEP:]
    j = pl.program_id(1)

    n_lane_tiles = NSA_KV_WIDTH // LANES

    @pl.when(j < n_steps)
    def _():
        for i in range(PAGES_PER_STEP):
            x = page_refs[i][0].T
            for c in range(n_lane_tiles):
                rows_ref[c, i * PAGE_SIZE:(i + 1) * PAGE_SIZE, :] = x[:, c * LANES:(c + 1) * LANES]

    @pl.when(j == n_steps)
    def _():
        rows_ref[...] = jnp.zeros(rows_ref.shape, F32)
        for c in range(n_lane_tiles):
            rows_ref[c, 0:8, :] = new_ref[0, :, c * LANES:(c + 1) * LANES]

    n_chunks = PAGES_PER_STEP * PAGE_SIZE // CMP_STRIDE
    for p in range(CMP_STRIDE):
        for c in range(n_lane_tiles):
            o_ref[0, :, p * NSA_KV_WIDTH + c * LANES:p * NSA_KV_WIDTH + (c + 1) * LANES] = rows_ref[
                c, pl.ds(p, n_chunks, stride=CMP_STRIDE), :].astype(BF16)


def _gather_sel_kernel(pt_ref, *refs, n_steps):
    page_refs = refs[:PAGES_PER_STEP]
    newk_ref, newv_ref, ka_ref, vot_ref = refs[PAGES_PER_STEP:]
    j = pl.program_id(1)
    tag_row = lax.broadcasted_iota(jnp.int32, (HEAD_DIM, PAGE_SIZE), 0)
    tag_col = jnp.right_shift(lax.broadcasted_iota(jnp.int32, (HEAD_DIM, PAGE_SIZE), 1), 6)
    pages_per_tile = SEL_TILE // PAGE_SIZE

    @pl.when(j < n_steps)
    def _():
        for i in range(PAGES_PER_STEP):
            xt = page_refs[i][0]
            tag = jnp.where(tag_row == (PAGE_SIZE // SEL_BLOCK) * (i % pages_per_tile) + tag_col, 1.0, 0.0)
            rs = slice(i * PAGE_SIZE, (i + 1) * PAGE_SIZE)
            for g in range(NSA_GROUPS):
                kt = xt[g * HEAD_DIM:(g + 1) * HEAD_DIM, :]
                vt = xt[(NSA_GROUPS + g) * HEAD_DIM:(NSA_GROUPS + g + 1) * HEAD_DIM, :]
                ka_ref[0, rs, g * LANES:(g + 1) * LANES] = jnp.concatenate([kt, tag], axis=0).T.astype(BF16)
                vot_ref[0, g * LANES:g * LANES + HEAD_DIM, rs] = vt.astype(BF16)
                vot_ref[0, g * LANES + HEAD_DIM:(g + 1) * LANES, rs] = jnp.ones((HEAD_DIM, PAGE_SIZE), BF16)

    @pl.when(j == n_steps)
    def _():
        ka_ref[...] = jnp.zeros(ka_ref.shape, BF16)
        vot_ref[...] = jnp.zeros(vot_ref.shape, BF16)
        ka_ref[0, 0:16, :] = newk_ref[0]
        vnew = newv_ref[0].astype(F32)
        vnew = jnp.concatenate([vnew, jnp.zeros((LANES - vnew.shape[0], NSA_PACK_WIDTH), F32)], axis=0)
        vot_ref[0, :, 0:LANES] = vnew.T.astype(BF16)


def _gather_nsa(cache_t, page_table, new_blocks, kernel_fn, outs, scratch, name):
    b, n_pages = page_table.shape
    assert n_pages % PAGES_PER_STEP == 0
    n_steps = n_pages // PAGES_PER_STEP

    def page_spec(i):
        return pl.BlockSpec((1, NSA_KV_WIDTH, PAGE_SIZE),
                            lambda s, j, pt: (pt[s, jnp.minimum(j * PAGES_PER_STEP + i, n_pages - 1)], 0, 0))

    def out_spec(shape, axis):
        return pl.BlockSpec((1,) + shape, lambda s, j, pt: (s, j, 0) if axis == 0 else (s, 0, j))

    def out_shape(shape, axis):
        full = list(shape)
        full[axis] *= n_steps + 1
        return jax.ShapeDtypeStruct((b,) + tuple(full), BF16)

    in_specs = [page_spec(i) for i in range(PAGES_PER_STEP)]
    in_specs += [pl.BlockSpec((1,) + nb.shape[1:], lambda s, j, pt: (s, 0, 0)) for nb in new_blocks]
    return pl.pallas_call(
        functools.partial(kernel_fn, n_steps=n_steps),
        grid_spec=pltpu.PrefetchScalarGridSpec(
            num_scalar_prefetch=1, grid=(b, n_steps + 1), in_specs=in_specs,
            out_specs=[out_spec(s, a) for s, a in outs], scratch_shapes=scratch),
        out_shape=[out_shape(s, a) for s, a in outs],
        compiler_params=_cparams(("arbitrary", "arbitrary")),
        name=name,
    )(page_table, *([cache_t] * PAGES_PER_STEP), *new_blocks)


def _nsa_cache_pages(cache_l):
    n_pool = cache_l.shape[0]
    return cache_l.transpose(0, 2, 3, 4, 1).reshape(n_pool, NSA_KV_WIDTH, PAGE_SIZE)


def _nsa_gates(y, off):
    b, t, _ = y.shape
    g = y[:, :, off + _OFF_GA:off + _OFF_GA + 3 * NSA_HEADS].reshape(b, t, 3, NSA_GROUPS, NSA_REP)
    g = g.transpose(0, 3, 1, 2, 4).reshape(b, NSA_GROUPS, t, 3 * NSA_REP)
    return jnp.pad(g, ((0, 0), (0, 0), (0, 0), (0, LANES - 3 * NSA_REP)))


def kernel(x_prompt, x_sample, cache_cmp_kv, cache_sel_kv, cache_sb_kv, state_win_kv, page_table, rel_bias,
           w_in, cmp_pe, cmp_w1, cmp_w2, w_up_nsa, w_up_sb, w_out, ln1_g, ln1_b,
           w_router, b_router, w_gate_up, b_gate_up, w_down, b_down, ln2_g, ln2_b):
    depth, d_model = w_in.shape[0], w_in.shape[1]
    bp, seq, _ = x_prompt.shape
    bs, dec_seq, _ = x_sample.shape
    assert bp == 1 and dec_seq == 1
    n_pages = page_table.shape[1]
    past = n_pages * PAGE_SIZE
    n_buf = state_win_kv.shape[2]
    n_experts = w_router.shape[2]
    n_pool = cache_cmp_kv.shape[1]
    alpha = (2 * depth) ** 0.25
    off = 2 * d_model
    tq_s = 16
    assert seq % SEL_TILE == 0 and past % SEL_TILE == 0

    hp = x_prompt.reshape(seq, d_model)
    hs = x_sample.reshape(bs, d_model)
    outs = [[] for _ in range(8)]
    for l in range(depth):
        w_proj = _proj_weight(w_in[l], d_model)
        cmp_w = _compress_weights(cmp_pe[l], cmp_w1[l], cmp_w2[l])
        fin_w = (w_up_nsa[l].astype(BF16), w_up_sb[l].astype(BF16), w_out[l].astype(BF16),
                 ln1_g[l][None], ln1_b[l][None],
                 jnp.pad(w_router[l], ((0, 0), (0, LANES - n_experts))).astype(BF16),
                 jnp.pad(b_router[l], (0, LANES - n_experts))[None])
        moe_w = (w_gate_up[l][:, :, 0::2].astype(BF16), w_gate_up[l][:, :, 1::2].astype(BF16),
                 b_gate_up[l][:, None, 0::2], b_gate_up[l][:, None, 1::2],
                 w_down[l].astype(BF16), b_down[l][:, None, :], ln2_g[l][None], ln2_b[l][None])

        tm = min(512, seq)
        y, yb = _in_proj(hp, w_proj, tm, PROJ_TN)
        kv_c = y[:, off + _OFF_KVC:off + _OFF_KVC + NSA_KV_WIDTH]
        kv_s = y[:, off + _OFF_KVS:off + _OFF_KVS + NSA_KV_WIDTH]
        kv_w = y[:, off + _OFF_KVW:off + _OFF_KVW + NSA_KV_WIDTH]
        kv_b = y[:, off + _OFF_KVB:off + _OFF_KVB + 2 * SB_WIDTH]
        q_pk, ka, vot, win_pk = _pack(yb, off, tm, 0, 1, True)
        n_chunks = seq // CMP_STRIDE
        chunks = yb[:, off + _OFF_KVC:off + _OFF_KVC + NSA_KV_WIDTH].reshape(1, n_chunks, CHUNK_WIDTH)
        cmpkv = _compress(chunks, seq, cmp_w)
        o_a = _nsa_attention(rel_bias, q_pk[None], _nsa_gates(y[None], off), cmpkv, ka[None], vot[None], win_pk[None],
                             n_real=KEY_TILE, q_pos0=0, win_pos0=0, n_sel_rows=seq, skip_empty=False)
        o_b = _sb_attention(yb[None], off, tq=KEY_TILE)
        h1, comb = _finish(o_a[0], o_b[0], y, hp, fin_w, tm=min(256, seq), d_model=d_model, alpha=alpha,
                           n_experts=n_experts)
        hp = _moe(h1, comb, moe_w, tm=min(512, seq), alpha=alpha)
        n_win = min(WINDOW, seq)
        outs[0].append(kv_c.reshape(1, seq, 2, NSA_GROUPS, HEAD_DIM))
        outs[1].append(kv_s.reshape(1, seq, 2, NSA_GROUPS, HEAD_DIM))
        outs[2].append(kv_b.reshape(1, seq, 2, SB_HEADS, SB_HEAD_DIM))
        outs[3].append(kv_w[seq - n_win:].reshape(1, n_win, 2, NSA_GROUPS, HEAD_DIM))

        ys, ysb = _in_proj(hs, w_proj, bs, PROJ_TN)
        kv_c = ys[:, off + _OFF_KVC:off + _OFF_KVC + NSA_KV_WIDTH]
        kv_s = ys[:, off + _OFF_KVS:off + _OFF_KVS + NSA_KV_WIDTH]
        kv_w = ys[:, off + _OFF_KVW:off + _OFF_KVW + NSA_KV_WIDTH]
        kv_b = ys[:, off + _OFF_KVB:off + _OFF_KVB + 2 * SB_WIDTH]
        q_pk, ka_new, vo_new, _ = _pack(ysb, off, bs, past, 0, False)
        rows_step = PAGES_PER_STEP * PAGE_SIZE
        (chunks,) = _gather_nsa(_nsa_cache_pages(cache_cmp_kv[l]), page_table,
                                [_pad_rows(kv_c[:, None, :], 1, 8)], _gather_cmp_kernel,
                                [((rows_step // CMP_STRIDE, CHUNK_WIDTH), 0)],
                                [pltpu.VMEM((NSA_KV_WIDTH // LANES, rows_step, LANES), F32)], "gather_cmp")
        ka, vot = _gather_nsa(_nsa_cache_pages(cache_sel_kv[l]), page_table,
                              [_pad_rows(ka_new[:, None, :], 1, 16), _pad_rows(vo_new[:, None, :], 1, 16)],
                              _gather_sel_kernel, [((rows_step, NSA_PACK_WIDTH), 0), ((NSA_PACK_WIDTH, rows_step), 1)],
                              [], "gather_sel")
        cmpkv = _compress(chunks, past + 1, cmp_w)
        win_all = jnp.concatenate([state_win_kv[l].reshape(bs, n_buf, NSA_KV_WIDTH), kv_w[:, None, :]], axis=1)
        win_pk = win_all.reshape(bs, n_buf + 1, 2, NSA_GROUPS, HEAD_DIM).transpose(0, 1, 3, 2, 4).reshape(
            bs, n_buf + 1, NSA_PACK_WIDTH).astype(BF16)
        win_pk = _pad_rows(win_pk, 1, _round_up(n_buf + 1, KEY_TILE))
        o_a = _nsa_attention(rel_bias, _pad_rows(q_pk[:, None, :], 1, KEY_TILE),
                             _pad_rows(_nsa_gates(ys[:, None, :], off), 2, KEY_TILE),
                             cmpkv, ka, vot, win_pk, n_real=1, q_pos0=past, win_pos0=past - n_buf,
                             n_sel_rows=past + 1, skip_empty=True)
        q_b = ysb[:, off + _OFF_QB:off + _OFF_QB + SB_WIDTH].reshape(bs, SB_HEADS, 1, SB_HEAD_DIM)
        o_b = _sb_decode(cache_sb_kv[l].reshape(n_pool, PAGE_SIZE * 2 * SB_HEADS, SB_HEAD_DIM), page_table,
                         _pad_rows(q_b, 2, tq_s))
        h1, comb = _finish(o_a[:, 0], o_b.reshape(bs, SB_WIDTH).astype(BF16), ys, hs, fin_w, tm=bs,
                           d_model=d_model, alpha=alpha, n_experts=n_experts)
        hs = _moe(h1, comb, moe_w, tm=bs, alpha=alpha)
        outs[4].append(kv_c.reshape(bs, 1, 2, NSA_GROUPS, HEAD_DIM))
        outs[5].append(kv_s.reshape(bs, 1, 2, NSA_GROUPS, HEAD_DIM))
        outs[6].append(kv_b.reshape(bs, 1, 2, SB_HEADS, SB_HEAD_DIM))
        outs[7].append(win_all[:, 1:].reshape(bs, n_buf, 2, NSA_GROUPS, HEAD_DIM))

    return (hp.reshape(bp, seq, d_model), hs.reshape(bs, dec_seq, d_model)) + tuple(jnp.stack(o) for o in outs)
```

```python
import functools
import math

import numpy as np
import jax
import jax.numpy as jnp
from jax import lax
from jax.experimental import pallas as pl
from jax.experimental.pallas import tpu as pltpu

F32 = jnp.float32
BF16 = jnp.bfloat16

NSA_HEADS = 16
NSA_GROUPS = 4
NSA_REP = NSA_HEADS // NSA_GROUPS
HEAD_DIM = 64
CMP_BLOCK = 32
CMP_STRIDE = 16
CMP_HIDDEN = 64
SEL_BLOCK = 64
SEL_TOPN = 16
WINDOW = 512
SB_HEADS = 8
SB_HEAD_DIM = 128
REL_BUCKETS = 32
REL_MAX_DIST = 4096
TOP_K = 4
SWIGLU_LIMIT = 7.0
SWIGLU_ALPHA = 1.702
LN_EPS = 1e-5
NEG_INF = -1e30
FORCED_SCORE = 1e4
PAGE_SIZE = 128

LANES = 128
KEY_TILE = 128
SEL_TILE = 512
SEL_TILE_BLOCKS = SEL_TILE // SEL_BLOCK
NSA_WIDTH = NSA_HEADS * HEAD_DIM
SB_WIDTH = SB_HEADS * SB_HEAD_DIM
NSA_KV_WIDTH = 2 * NSA_GROUPS * HEAD_DIM
NSA_PACK_WIDTH = NSA_GROUPS * LANES
CHUNK_WIDTH = CMP_STRIDE * NSA_KV_WIDTH
CMP_PRE_WIDTH = 2 * NSA_GROUPS * CMP_HIDDEN
VMEM_LIMIT = 56 * 1024 * 1024
PAGES_PER_STEP = 8
SB_PAGES_PER_STEP = 4
SB_PROMPT_TQ = 256
EXP_UNDERFLOW = -104.0

_MAX_EXACT = REL_BUCKETS // 2
_T5_THRESH = tuple(int(math.ceil(_MAX_EXACT * 2.0 ** (j / 2.0) - 1e-9)) for j in range(1, REL_BUCKETS - _MAX_EXACT))
_FAR_DIST = _T5_THRESH[-1]
N_BIAS_TILES = -(-(_FAR_DIST + KEY_TILE) // KEY_TILE) + 1
WIN_TILES = WINDOW // KEY_TILE + 1
CMP_TAB_M_MAX = (_FAR_DIST + CMP_BLOCK - 1 + CMP_STRIDE * (LANES - 1) - 1) // KEY_TILE
CMP_TAB_X0 = -(-(KEY_TILE // CMP_STRIDE) * CMP_TAB_M_MAX // LANES) * LANES
CMP_TAB_WIDTH = CMP_TAB_X0 + 2 * LANES

_OFF_QA = 0
_OFF_KVC = _OFF_QA + NSA_WIDTH
_OFF_KVS = _OFF_KVC + NSA_KV_WIDTH
_OFF_KVW = _OFF_KVS + NSA_KV_WIDTH
_OFF_GA = _OFF_KVW + NSA_KV_WIDTH
_OFF_QB = _OFF_GA + LANES
_OFF_KVB = _OFF_QB + SB_WIDTH
_OFF_END = _OFF_KVB + 2 * SB_WIDTH
PROJ_TN = 768


def _cparams(sem):
    return pltpu.CompilerParams(dimension_semantics=sem, vmem_limit_bytes=VMEM_LIMIT)


def _round_up(x, m):
    return -(-x // m) * m


def _dot(a, b):
    return jnp.dot(a, b, preferred_element_type=F32)


def _dot_nt(a, b):
    return lax.dot_general(a, b, (((1,), (1,)), ((), ())), preferred_element_type=F32)


def _layer_norm(x, g, b):
    mu = jnp.mean(x, axis=-1, keepdims=True)
    xc = x - mu
    var = jnp.mean(xc * xc, axis=-1, keepdims=True)
    return xc * lax.rsqrt(var + LN_EPS) * g + b


def _pad_rows(x, axis, n):
    pad = [(0, 0)] * x.ndim
    pad[axis] = (0, n - x.shape[axis])
    return jnp.pad(x, pad)


def _matmul_kernel(x_ref, w_ref, o_ref, ob_ref, xb_ref):
    @pl.when(pl.program_id(1) == 0)
    def _():
        xb_ref[...] = x_ref[...].astype(BF16)

    acc = _dot(xb_ref[...], w_ref[...])
    o_ref[...] = acc
    ob_ref[...] = acc.astype(BF16)


def _in_proj(x, w, tm, tn):
    m, k = x.shape
    n = w.shape[1]
    return pl.pallas_call(
        _matmul_kernel,
        grid=(m // tm, n // tn),
        in_specs=[pl.BlockSpec((tm, k), lambda i, j: (i, 0)),
                  pl.BlockSpec((k, tn), lambda i, j: (0, j))],
        out_specs=[pl.BlockSpec((tm, tn), lambda i, j: (i, j)),
                   pl.BlockSpec((tm, tn), lambda i, j: (i, j))],
        out_shape=[jax.ShapeDtypeStruct((m, n), F32), jax.ShapeDtypeStruct((m, n), BF16)],
        scratch_shapes=[pltpu.VMEM((tm, k), BF16)],
        compiler_params=_cparams(("arbitrary", "arbitrary")),
        name="in_proj",
    )(x, w)


def _proj_weight(w_in_l, d_model):
    o = np.cumsum((0, NSA_WIDTH, NSA_KV_WIDTH, NSA_KV_WIDTH, NSA_KV_WIDTH, 3 * NSA_HEADS,
                   SB_WIDTH, 2 * SB_WIDTH, 2 * d_model))
    g_m = w_in_l[:, o[7]:o[8]]
    head = w_in_l[:, o[0]:o[5]]
    tail = w_in_l[:, o[5]:o[7]]
    pad_a = jnp.zeros((d_model, LANES - 3 * NSA_HEADS), w_in_l.dtype)
    n = 2 * d_model + _OFF_END
    n_pad = _round_up(n, PROJ_TN)
    pad_b = jnp.zeros((d_model, n_pad - n), w_in_l.dtype)
    return jnp.concatenate([g_m, head, pad_a, tail, pad_b], axis=1).astype(BF16)


def _pack_kernel(qa_ref, qb_ref, ks_ref, kw_ref, pq_ref, pk_ref, pv_ref,
                 q_ref, ka_ref, vo_ref, wk_ref, wvo_ref, *, tm, pos0, pos_stride, transpose_vo):
    q_ref[:, 0:NSA_HEADS * LANES // 2] = _dot(qa_ref[...], pq_ref[...]).astype(BF16)
    q_ref[:, NSA_HEADS * LANES // 2:] = _dot(qb_ref[...], pq_ref[...]).astype(BF16)
    lane = lax.broadcasted_iota(jnp.int32, (tm, NSA_PACK_WIDTH), 1) % LANES
    pos = pos0 + pos_stride * (pl.program_id(0) * tm + lax.broadcasted_iota(jnp.int32, (tm, NSA_PACK_WIDTH), 0))
    local_blk = jnp.right_shift(pos % SEL_TILE, 6)
    tag = (lane >= HEAD_DIM) & (lane - HEAD_DIM == local_blk)
    ka_ref[...] = jnp.where(tag, 1.0, _dot(ks_ref[...], pk_ref[...])).astype(BF16)
    vo = jnp.where(lane >= HEAD_DIM, 1.0, _dot(ks_ref[...], pv_ref[...]))
    vo_ref[...] = (vo.T if transpose_vo else vo).astype(BF16)
    wk_ref[...] = _dot(kw_ref[...], pk_ref[...]).astype(BF16)
    wvo = jnp.where(lane >= HEAD_DIM, 1.0, _dot(kw_ref[...], pv_ref[...]))
    wvo_ref[...] = (wvo.T if transpose_vo else wvo).astype(BF16)


def _pack_matrices():
    half = NSA_WIDTH // 2
    pq = np.zeros((half, half * 2), np.float32)
    for h in range(NSA_HEADS // 2):
        for d in range(HEAD_DIM):
            pq[h * HEAD_DIM + d, h * LANES + d] = HEAD_DIM ** -0.5
    pk = np.zeros((NSA_KV_WIDTH, NSA_PACK_WIDTH), np.float32)
    pv = np.zeros((NSA_KV_WIDTH, NSA_PACK_WIDTH), np.float32)
    for g in range(NSA_GROUPS):
        for d in range(HEAD_DIM):
            pk[g * HEAD_DIM + d, g * LANES + d] = 1.0
            pv[NSA_GROUPS * HEAD_DIM + g * HEAD_DIM + d, g * LANES + d] = 1.0
    return tuple(jnp.asarray(a, dtype=BF16) for a in (pq, pk, pv))


def _pack(yb, off, tm, pos0, pos_stride, transpose_vo):
    m = yb.shape[0]
    assert off % NSA_KV_WIDTH == 0 and SEL_TILE_BLOCKS <= LANES - HEAD_DIM
    c0 = off // NSA_KV_WIDTH
    mats = _pack_matrices()
    blk = lambda c: pl.BlockSpec((tm, NSA_KV_WIDTH), lambda i, c=c: (i, c))
    full = lambda a: pl.BlockSpec(a.shape, lambda i: (0, 0))
    rows_spec = lambda w: pl.BlockSpec((tm, w), lambda i: (i, 0))
    rows_shape = lambda w: jax.ShapeDtypeStruct((m, w), BF16)
    vo_spec = pl.BlockSpec((NSA_PACK_WIDTH, tm), lambda i: (0, i)) if transpose_vo else rows_spec(NSA_PACK_WIDTH)
    vo_shape = jax.ShapeDtypeStruct((NSA_PACK_WIDTH, m), BF16) if transpose_vo else rows_shape(NSA_PACK_WIDTH)
    return pl.pallas_call(
        functools.partial(_pack_kernel, tm=tm, pos0=pos0, pos_stride=pos_stride, transpose_vo=transpose_vo),
        grid=(m // tm,),
        in_specs=[blk(c0), blk(c0 + 1), blk(c0 + _OFF_KVS // NSA_KV_WIDTH), blk(c0 + _OFF_KVW // NSA_KV_WIDTH)]
        + [full(a) for a in mats],
        out_specs=[rows_spec(NSA_HEADS * LANES), rows_spec(NSA_PACK_WIDTH), vo_spec, rows_spec(NSA_PACK_WIDTH), vo_spec],
        out_shape=[rows_shape(NSA_HEADS * LANES), rows_shape(NSA_PACK_WIDTH), vo_shape, rows_shape(NSA_PACK_WIDTH),
                   vo_shape],
        compiler_params=_cparams(("arbitrary",)),
        name="nsa_pack",
    )(yb, yb, yb, yb, *mats)


def _compress_kernel(x_ref, pe_ref, w1_ref, w2_ref, o_ref, acc_ref, pacc_ref, *, nk, nb):
    kk = pl.program_id(1)

    @pl.when(kk == 0)
    def _():
        acc_ref[...] = jnp.zeros_like(acc_ref)
        pacc_ref[...] = jnp.zeros_like(pacc_ref)

    w1 = w1_ref[...]
    acc_ref[...] += _dot(x_ref[0], w1)
    pacc_ref[...] += _dot(pe_ref[...].astype(BF16), w1)

    @pl.when(kk == nk - 1)
    def _():
        h = CMP_PRE_WIDTH
        pe_bias = pacc_ref[0:1, 0:h] + pacc_ref[1:2, h:2 * h]
        pre = acc_ref[0:nb, 0:h] + acc_ref[1:nb + 1, h:2 * h] + pe_bias
        act = jax.nn.gelu(pre)
        o_ref[0] = _dot(act.astype(BF16), w2_ref[...]).astype(BF16)


def _compress_weights(cmp_pe_l, cmp_w1_l, cmp_w2_l):
    halves = CMP_BLOCK // CMP_STRIDE
    w1h = cmp_w1_l.reshape(2, halves, CMP_STRIDE, HEAD_DIM, CMP_HIDDEN)
    eye_k = jnp.eye(2, dtype=F32)
    eye_g = jnp.eye(NSA_GROUPS, dtype=F32)
    w1big = jnp.einsum('khpdf,kK,gG->pKGdhkgf', w1h, eye_k, eye_g).reshape(CHUNK_WIDTH, halves * CMP_PRE_WIDTH)
    w2big = jnp.einsum('kfd,kK,gG->kgfGKd', cmp_w2_l, eye_k, eye_g).reshape(CMP_PRE_WIDTH, NSA_PACK_WIDTH)
    pe = cmp_pe_l.reshape(2, halves, CMP_STRIDE, HEAD_DIM).transpose(1, 2, 0, 3)
    pe = jnp.broadcast_to(pe[:, :, :, None, :], (halves, CMP_STRIDE, 2, NSA_GROUPS, HEAD_DIM))
    pe_rows = jnp.pad(pe.reshape(halves, CHUNK_WIDTH), ((0, 8 - halves), (0, 0)))
    return pe_rows, w1big.astype(BF16), w2big.astype(BF16)


def _compress(chunks, n_rows, weights):
    pe_rows, w1big, w2big = weights
    b = chunks.shape[0]
    n_chunks = -(-n_rows // CMP_STRIDE)
    nb = _round_up(n_chunks - 1, LANES)
    ncp = nb + 16
    if chunks.shape[1] < ncp:
        chunks = _pad_rows(chunks, 1, ncp)
    kstep = 2048
    nk = CHUNK_WIDTH // kstep
    nw = w1big.shape[1]
    return pl.pallas_call(
        functools.partial(_compress_kernel, nk=nk, nb=nb),
        grid=(b, nk),
        in_specs=[pl.BlockSpec((1, ncp, kstep), lambda i, k: (i, 0, k)),
                  pl.BlockSpec((8, kstep), lambda i, k: (0, k)),
                  pl.BlockSpec((kstep, nw), lambda i, k: (k, 0)),
                  pl.BlockSpec(w2big.shape, lambda i, k: (0, 0))],
        out_specs=pl.BlockSpec((1, nb, NSA_PACK_WIDTH), lambda i, k: (i, 0, 0)),
        out_shape=jax.ShapeDtypeStruct((b, nb, NSA_PACK_WIDTH), BF16),
        scratch_shapes=[pltpu.VMEM((ncp, nw), F32), pltpu.VMEM((8, nw), F32)],
        compiler_params=_cparams(("arbitrary", "arbitrary")),
        name="nsa_compress",
    )(chunks, pe_rows, w1big, w2big)


def _t5_bucket(d):
    n = jnp.maximum(d, 0)
    cnt = jnp.zeros_like(n)
    for th in _T5_THRESH:
        cnt = cnt + jnp.where(n >= th, 1, 0)
    return jnp.where(n < _MAX_EXACT, n, _MAX_EXACT + cnt)


def _bias_tiles(d, rel_ref, g):
    bucket = _t5_bucket(d)
    outs = [jnp.zeros(d.shape, F32) for _ in range(NSA_REP)]
    for j in range(REL_BUCKETS):
        hit = bucket == j
        for r in range(NSA_REP):
            outs[r] = jnp.where(hit, rel_ref[j, NSA_REP * g + r], outs[r])
    return tuple(outs)


def _sel_head_step(st, shift, cs, mt_ref, acct_ref, pt_ref):
    m_old = mt_ref[:, cs]
    m_new = jnp.maximum(m_old, jnp.max(st, axis=0, keepdims=True) + shift)
    acct_ref[:, cs] = jnp.exp(m_old - m_new) * acct_ref[:, cs]
    pt_ref[0:st.shape[0], cs] = jnp.exp(st - (m_new - shift)).astype(BF16)
    mt_ref[:, cs] = m_new


def _nsa_kernel(rel_ref, q_ref, gate_ref, cmp_ref, ka_ref, vot_ref, *rest,
                tq, n_real, q_pos0, win_pos0, nb, nsbp, n_sb, ls, lw, skip_empty):
    wk_refs = rest[:WIN_TILES]
    wvt_refs = rest[WIN_TILES:2 * WIN_TILES]
    (ov_ref, o_ref, tabt_ref, ctab_ref, wbias_ref, s_ref, pent_ref, qat_ref, st_ref, pt_ref, st2_ref, pt2_ref,
     acct_ref, mt_ref) = rest[2 * WIN_TILES:]
    g = pl.program_id(0)
    b = pl.program_id(1)
    qt = pl.program_id(2)
    t0 = q_pos0 + qt * tq
    rows = lax.broadcasted_iota(jnp.int32, (tq, LANES), 0)
    cols = lax.broadcasted_iota(jnp.int32, (tq, LANES), 1)
    tpos = t0 + rows
    far_idx = N_BIAS_TILES - 1

    @pl.when((b == 0) & (qt == 0))
    def _():
        def body(di, c):
            outs_t = _bias_tiles(di * KEY_TILE + cols - rows, rel_ref, g)
            for r in range(NSA_REP):
                tabt_ref[di, r] = outs_t[r]
            return c
        lax.fori_loop(0, N_BIAS_TILES, body, 0)
        for i in range(WIN_TILES):
            d = (WIN_TILES - 1 - i) * KEY_TILE + cols - rows
            outs = _bias_tiles(d, rel_ref, g)
            ok = (d >= 0) & (d < WINDOW)
            for r in range(NSA_REP):
                wbias_ref[r, i * KEY_TILE:(i + 1) * KEY_TILE, :] = jnp.where(ok, outs[r], NEG_INF)
        for xt in range(CMP_TAB_WIDTH // LANES):
            d = rows - (CMP_BLOCK - 1) + CMP_STRIDE * (CMP_TAB_X0 - (xt * LANES + cols))
            outs = _bias_tiles(d, rel_ref, g)
            for r in range(NSA_REP):
                ctab_ref[r, :, xt * LANES:(xt + 1) * LANES] = outs[r]

    qs = [q_ref[0, :, r * LANES:(r + 1) * LANES] for r in range(NSA_REP)]
    q = jnp.concatenate(qs, axis=0)
    row_slices = [pl.ds(r * tq, tq) for r in range(NSA_REP)]

    for ct in range(nb // LANES):
        kc = cmp_ref[0, ct * LANES:(ct + 1) * LANES, :]
        s = _dot_nt(q, kc)
        d = tpos - (CMP_BLOCK - 1) - CMP_STRIDE * (ct * LANES + cols)
        m_idx = jnp.right_shift(t0, 7) - (LANES * CMP_STRIDE // KEY_TILE) * ct

        def far_fn():
            return tuple(jnp.full((tq, LANES), rel_ref[REL_BUCKETS - 1, NSA_REP * g + r], F32)
                         for r in range(NSA_REP))

        def near_fn(m_idx=m_idx):
            start = CMP_TAB_X0 - (KEY_TILE // CMP_STRIDE) * m_idx
            a0 = pl.multiple_of(jnp.right_shift(start, 7) * LANES, LANES)
            shift = (2 * LANES - (start - a0)) % (2 * LANES)
            return tuple(pltpu.roll(ctab_ref[r, :, pl.ds(a0, 2 * LANES)], shift, 1)[:, :LANES]
                         for r in range(NSA_REP))

        bias = lax.cond((m_idx >= 0) & (m_idx <= CMP_TAB_M_MAX), near_fn, far_fn)
        valid = d >= 0
        for r in range(NSA_REP):
            s_ref[row_slices[r], ct * LANES:(ct + 1) * LANES] = jnp.where(
                valid, s[r * tq:(r + 1) * tq] + bias[r], NEG_INF)

    s = s_ref[:, 0:nb]
    valid = s > 0.5 * NEG_INF
    p = jnp.where(valid, jnp.exp(s - jnp.max(s, axis=1, keepdims=True)), 0.0)
    pc = p / jnp.maximum(jnp.sum(p, axis=1, keepdims=True), 1e-30)
    o_cmp = _dot(pc.astype(BF16), cmp_ref[0])

    psum = pc[0:tq] + pc[tq:2 * tq] + pc[2 * tq:3 * tq] + pc[3 * tq:4 * tq]
    p_hi = psum.astype(BF16)
    p_lo = (psum - p_hi.astype(F32)).astype(BF16)
    imp = _dot(p_hi, ov_ref[...]) + _dot(p_lo, ov_ref[...])
    blk = lax.broadcasted_iota(jnp.int32, (tq, nsbp), 1)
    blk_f = blk.astype(F32)
    sel_rows = lax.broadcasted_iota(jnp.int32, (tq, nsbp), 0)
    qb = jnp.right_shift(t0 + sel_rows, 6)
    forced = (blk == 0) | (blk == qb) | (blk == qb - 1)
    score = jnp.where(forced, FORCED_SCORE, jnp.where(blk <= qb, imp, -1.0))
    score = jnp.where(blk < n_sb, score, -2.0)
    score = score.T
    blk_t = lax.broadcasted_iota(jnp.int32, (nsbp, tq), 0).astype(F32)
    sel_t = jnp.zeros((nsbp, tq), F32)
    for _ in range(min(SEL_TOPN, n_sb)):
        mx = jnp.max(score, axis=0, keepdims=True)
        first = jnp.min(jnp.where(score == mx, blk_t, 1e9), axis=0, keepdims=True)
        pick = blk_t == first
        sel_t = jnp.where(pick, 1.0, sel_t)
        score = jnp.where(pick, -3.0, score)
    if n_real < tq:
        real_q = lax.broadcasted_iota(jnp.int32, (nsbp, tq), 1) < n_real
        sel_t = jnp.where(real_q, sel_t, 0.0)
    pent_ref[...] = (sel_t - 1.0) * 1e30

    mt_ref[...] = jnp.full(mt_ref.shape, NEG_INF, F32)
    acct_ref[...] = jnp.zeros(acct_ref.shape, F32)
    qat_ref[...] = q.astype(F32).T
    col_slices = [slice(r * tq, (r + 1) * tq) for r in range(NSA_REP)]

    kb_first = (q_pos0 - win_pos0) // KEY_TILE - (WIN_TILES - 1) + qt
    st_ref[...] = _dot(jnp.concatenate([wk_refs[i][0] for i in range(WIN_TILES)], axis=0),
                       qat_ref[...].astype(BF16))
    for r in range(NSA_REP):
        parts = []
        for i in range(WIN_TILES):
            ks = slice(i * KEY_TILE, (i + 1) * KEY_TILE)
            in_range = (kb_first + i >= 0) & (kb_first + i < lw // KEY_TILE)
            parts.append(jnp.where(in_range, st_ref[ks, col_slices[r]] + wbias_ref[r, ks, :], NEG_INF))
        st = jnp.concatenate(parts, axis=0)
        pt_ref[:, col_slices[r]] = jnp.exp(st - jnp.max(st, axis=0, keepdims=True)).astype(BF16)
    acc_w = _dot(jnp.concatenate([wvt_refs[i][0] for i in range(WIN_TILES)], axis=1), pt_ref[...])
    o_win_t = acc_w[0:HEAD_DIM] / jnp.maximum(acc_w[HEAD_DIM:], 1e-30)

    key_i = lax.broadcasted_iota(jnp.int32, (SEL_TILE, tq), 0)
    qry_i = lax.broadcasted_iota(jnp.int32, (SEL_TILE, tq), 1)
    real_pen = lax.broadcasted_iota(jnp.int32, (SEL_TILE_BLOCKS, tq), 1) < n_real

    def tile_pen(kt):
        return pent_ref[pl.ds(pl.multiple_of(kt * SEL_TILE_BLOCKS, SEL_TILE_BLOCKS), SEL_TILE_BLOCKS), :]

    def tile_scores(kt, pen, slot):
        for r in range(NSA_REP):
            qat_ref[HEAD_DIM:HEAD_DIM + SEL_TILE_BLOCKS, col_slices[r]] = pen
        st2_ref[slot] = _dot(ka_ref[0, pl.ds(pl.multiple_of(kt * SEL_TILE, SEL_TILE), SEL_TILE), :],
                             qat_ref[...].astype(BF16))

    def tile_softmax(kt, slot, near):
        k0 = kt * SEL_TILE
        if near:
            di0 = jnp.right_shift(t0 - k0, 7)
            causal = k0 + key_i <= t0 + qry_i
        for r in range(NSA_REP):
            st = st2_ref[slot, :, col_slices[r]]
            if near:
                bias = jnp.concatenate(
                    [tabt_ref[jnp.clip(di0 - c4, 0, far_idx), r] for c4 in range(SEL_TILE // KEY_TILE)], axis=0)
                st = jnp.where(causal, st + bias, NEG_INF)
                shift = 0.0
            else:
                shift = rel_ref[REL_BUCKETS - 1, NSA_REP * g + r]
            _sel_head_step(st, shift, col_slices[r], mt_ref, acct_ref, pt2_ref.at[slot])

    def tile_values(kt, slot):
        acct_ref[...] += _dot(vot_ref[0, :, pl.ds(pl.multiple_of(kt * SEL_TILE, SEL_TILE), SEL_TILE)], pt2_ref[slot])

    n_kt = jnp.minimum(jnp.right_shift(t0 + tq - 1, 9) + 1, ls // SEL_TILE)
    n_far = jnp.clip(jnp.right_shift(t0 - _FAR_DIST - (SEL_TILE - 1), 9) + 1, 0, n_kt)

    if skip_empty:
        def sel_body(kt, c):
            pen = tile_pen(kt)

            @pl.when(jnp.max(jnp.where(real_pen, pen, NEG_INF)) > -1.0)
            def _():
                tile_scores(kt, pen, 0)
                pl.when(kt < n_far)(lambda: tile_softmax(kt, 0, False))
                pl.when(kt >= n_far)(lambda: tile_softmax(kt, 0, True))
                tile_values(kt, 0)
            return c
        lax.fori_loop(0, n_kt, sel_body, 0)
    else:
        def run(lo, hi, near):
            @pl.when(hi > lo)
            def _():
                pt2_ref[1] = jnp.zeros((SEL_TILE, NSA_REP * tq), BF16)
                tile_scores(lo, tile_pen(lo), 0)

                def body(p, c):
                    i = lo + 2 * p
                    nxt = jnp.minimum(i + 1, hi - 1)
                    tile_scores(nxt, tile_pen(nxt), 1)
                    tile_values(jnp.maximum(i - 1, lo), 1)
                    tile_softmax(i, 0, near)

                    @pl.when(i + 1 < hi)
                    def _():
                        nx2 = jnp.minimum(i + 2, hi - 1)
                        tile_scores(nx2, tile_pen(nx2), 0)
                        tile_values(i, 0)
                        tile_softmax(i + 1, 1, near)
                    return c
                lax.fori_loop(0, jnp.right_shift(hi - lo + 1, 1), body, 0)
                last_odd = (hi - 1 - lo) % 2 == 1
                pl.when(last_odd)(lambda: tile_values(hi - 1, 1))
                pl.when(jnp.logical_not(last_odd))(lambda: tile_values(hi - 1, 0))

        run(0, n_far, False)
        run(n_far, n_kt, True)
    acct = acct_ref[...]
    o_sel_t = acct[0:HEAD_DIM] / jnp.maximum(acct[HEAD_DIM:], 1e-30)
    o_sel = jnp.concatenate([jnp.zeros_like(o_sel_t), o_sel_t], axis=0).T
    o_win = jnp.concatenate([jnp.zeros_like(o_win_t), o_win_t], axis=0).T

    gt = jax.nn.sigmoid(gate_ref[0, 0])
    heads = []
    for r in range(NSA_REP):
        rs = slice(r * tq, (r + 1) * tq)
        heads.append(gt[:, r:r + 1] * o_cmp[rs] + gt[:, NSA_REP + r:NSA_REP + r + 1] * o_sel[rs]
                     + gt[:, 2 * NSA_REP + r:2 * NSA_REP + r + 1] * o_win[rs])
    low = cols < HEAD_DIM
    for pair in range(NSA_REP // 2):
        left = pltpu.roll(heads[2 * pair], HEAD_DIM, 1)
        o_ref[0, :, pair * LANES:(pair + 1) * LANES] = jnp.where(low, left, heads[2 * pair + 1]).astype(BF16)


def _overlap_matrix(nb, nsbp):
    cs = np.arange(nb)[:, None] * CMP_STRIDE
    ss = np.arange(nsbp)[None, :] * SEL_BLOCK
    ov = np.minimum(cs + CMP_BLOCK, ss + SEL_BLOCK) - np.maximum(cs, ss)
    return jnp.asarray(np.maximum(ov, 0).astype(np.float32) / CMP_BLOCK, dtype=BF16)


def _nsa_attention(rel_bias, q, gates, cmpkv, ka, vot, wk, wvt, *, n_real, q_pos0, win_pos0, n_sel_rows,
                   skip_empty):
    b, tp, _ = q.shape
    nb = cmpkv.shape[1]
    ls = ka.shape[1]
    lw = wk.shape[1]
    tq = LANES
    n_sb = -(-n_sel_rows // SEL_BLOCK)
    nsbp = _round_up(n_sb, LANES)
    assert q_pos0 % SEL_TILE == 0 and win_pos0 % KEY_TILE == 0 and n_sb >= SEL_TOPN
    assert ls % SEL_TILE == 0 and lw % KEY_TILE == 0 and tp % tq == 0 and vot.shape[2] == ls and wvt.shape[2] == lw
    ov = _overlap_matrix(nb, nsbp)
    kern = functools.partial(_nsa_kernel, tq=tq, n_real=n_real, q_pos0=q_pos0, win_pos0=win_pos0, nb=nb,
                             nsbp=nsbp, n_sb=n_sb, ls=ls, lw=lw, skip_empty=skip_empty)
    kb_first = (q_pos0 - win_pos0) // KEY_TILE - (WIN_TILES - 1)
    win_blk = lambda t, i: jnp.clip(kb_first + t + i, 0, lw // KEY_TILE - 1)
    wk_specs = [pl.BlockSpec((1, KEY_TILE, LANES), lambda g, s, t, i=i: (s, win_blk(t, i), g))
                for i in range(WIN_TILES)]
    wvt_specs = [pl.BlockSpec((1, LANES, KEY_TILE), lambda g, s, t, i=i: (s, g, win_blk(t, i)))
                 for i in range(WIN_TILES)]
    rows_all = NSA_REP * tq
    return pl.pallas_call(
        kern,
        grid=(NSA_GROUPS, b, tp // tq),
        in_specs=[pl.BlockSpec(memory_space=pltpu.SMEM),
                  pl.BlockSpec((1, tq, NSA_REP * LANES), lambda g, i, t: (i, t, g)),
                  pl.BlockSpec((1, 1, tq, LANES), lambda g, i, t: (i, g, t, 0)),
                  pl.BlockSpec((1, nb, LANES), lambda g, i, t: (i, 0, g)),
                  pl.BlockSpec((1, ls, LANES), lambda g, i, t: (i, 0, g)),
                  pl.BlockSpec((1, LANES, ls), lambda g, i, t: (i, g, 0))]
        + wk_specs + wvt_specs + [pl.BlockSpec((nb, nsbp), lambda g, i, t: (0, 0))],
        out_specs=pl.BlockSpec((1, tq, NSA_REP * HEAD_DIM), lambda g, i, t: (i, t, g)),
        out_shape=jax.ShapeDtypeStruct((b, tp, NSA_WIDTH), BF16),
        scratch_shapes=[pltpu.VMEM((N_BIAS_TILES, NSA_REP, KEY_TILE, tq), F32),
                        pltpu.VMEM((NSA_REP, tq, CMP_TAB_WIDTH), F32),
                        pltpu.VMEM((NSA_REP, WIN_TILES * KEY_TILE, tq), F32),
                        pltpu.VMEM((rows_all, nb), F32),
                        pltpu.VMEM((nsbp, tq), F32),
                        pltpu.VMEM((LANES, rows_all), F32),
                        pltpu.VMEM((WIN_TILES * KEY_TILE, rows_all), F32),
                        pltpu.VMEM((WIN_TILES * KEY_TILE, rows_all), BF16),
                        pltpu.VMEM((2, SEL_TILE, rows_all), F32),
                        pltpu.VMEM((2, SEL_TILE, rows_all), BF16),
                        pltpu.VMEM((LANES, rows_all), F32),
                        pltpu.VMEM((1, rows_all), F32)],
        compiler_params=_cparams(("arbitrary", "arbitrary", "arbitrary")),
        name="nsa_attention",
    )(rel_bias, q, gates, cmpkv, ka, vot, *([wk] * WIN_TILES), *([wvt] * WIN_TILES), ov)


def _sb_tile(q, k, v, u, causal, carry):
    z = _dot_nt(q, k) * (SB_HEAD_DIM ** -0.5)
    softplus = jnp.maximum(z, 0.0) + jnp.log(1.0 + jnp.exp(-jnp.abs(z)))
    log_stay = -softplus if causal is None else jnp.where(causal, -softplus, 0.0)
    hi = log_stay.astype(BF16)
    lo = (log_stay - hi.astype(F32)).astype(BF16)
    sums = _dot(jnp.concatenate([hi, lo], axis=1), u)
    a = jnp.exp((z - softplus) + sums[:, :LANES] + carry)
    if causal is not None:
        a = jnp.where(causal, a, 0.0)
    return _dot(a.astype(BF16), v), carry + sums[:, LANES:]


def _sb_kernel(q_ref, k_ref, v_ref, u_ref, o_ref, acc_ref, car_ref, *, tq, q_pos0, lp):
    qt = pl.program_id(2)
    t0 = q_pos0 + qt * tq
    q = q_ref[0]
    rows = lax.broadcasted_iota(jnp.int32, (tq, LANES), 0)
    cols = lax.broadcasted_iota(jnp.int32, (tq, LANES), 1)
    tpos = t0 + rows
    acc_ref[...] = jnp.zeros(acc_ref.shape, F32)
    car_ref[...] = jnp.zeros(car_ref.shape, F32)

    def cond(c):
        kt, go = c
        return (kt >= 0) & go

    def body(c):
        kt, _ = c
        k0 = pl.multiple_of(kt * KEY_TILE, KEY_TILE)
        pv, carry = _sb_tile(q, k_ref[0, pl.ds(k0, KEY_TILE), :], v_ref[0, pl.ds(k0, KEY_TILE), :], u_ref[...],
                             k0 + cols < tpos, car_ref[...])
        acc_ref[...] += pv
        car_ref[...] = carry
        return kt - 1, jnp.max(carry) > EXP_UNDERFLOW

    kt_hi = jnp.minimum(jnp.right_shift(t0 + tq - 2, 7), lp // KEY_TILE - 1)
    lax.while_loop(cond, body, (kt_hi, True))
    o_ref[0] = acc_ref[...].astype(BF16)


def _suffix_matrix():
    j = np.arange(2 * KEY_TILE)[:, None] % KEY_TILE
    s = np.arange(2 * KEY_TILE)[None, :]
    u = np.where(s < KEY_TILE, j > s, True)
    return jnp.asarray(u.astype(np.float32), dtype=BF16)


def _sb_attention(yb, off, *, tq):
    _, t, _ = yb.shape
    assert t % KEY_TILE == 0 and t % tq == 0 and tq >= 2
    cq = (off + _OFF_QB) // SB_HEAD_DIM
    ck = (off + _OFF_KVB) // SB_HEAD_DIM
    cv = ck + SB_HEADS
    return pl.pallas_call(
        functools.partial(_sb_kernel, tq=tq, q_pos0=0, lp=t),
        grid=(1, SB_HEADS, t // tq),
        in_specs=[pl.BlockSpec((1, tq, SB_HEAD_DIM), lambda i, h, s: (i, s, cq + h)),
                  pl.BlockSpec((1, t, SB_HEAD_DIM), lambda i, h, s: (i, 0, ck + h)),
                  pl.BlockSpec((1, t, SB_HEAD_DIM), lambda i, h, s: (i, 0, cv + h)),
                  pl.BlockSpec((2 * KEY_TILE, 2 * KEY_TILE), lambda i, h, s: (0, 0))],
        out_specs=pl.BlockSpec((1, tq, SB_HEAD_DIM), lambda i, h, s: (i, s, h)),
        out_shape=jax.ShapeDtypeStruct((1, t, SB_WIDTH), BF16),
        scratch_shapes=[pltpu.VMEM((tq, SB_HEAD_DIM), F32), pltpu.VMEM((tq, LANES), F32)],
        compiler_params=_cparams(("arbitrary", "arbitrary", "arbitrary")),
        name="sb_attention",
    )(yb, yb, yb, _suffix_matrix())


def _sb_decode_kernel(pt_ref, q_ref, u_ref, *refs, n_steps):
    page_refs = refs[:SB_PAGES_PER_STEP]
    o_ref, acc_ref, car_ref, done_ref = refs[SB_PAGES_PER_STEP:]
    jj = pl.program_id(1)

    @pl.when(jj == 0)
    def _():
        acc_ref[...] = jnp.zeros(acc_ref.shape, F32)
        car_ref[...] = jnp.zeros(car_ref.shape, F32)
        done_ref[0] = 0

    for i in range(SB_PAGES_PER_STEP):
        @pl.when(done_ref[0] == 0)
        def _(i=i):
            mx = jnp.float32(2.0 * NEG_INF)
            for h in range(SB_HEADS):
                k = page_refs[i][0, pl.ds(h, PAGE_SIZE, stride=2 * SB_HEADS), :].astype(BF16)
                v = page_refs[i][0, pl.ds(SB_HEADS + h, PAGE_SIZE, stride=2 * SB_HEADS), :].astype(BF16)
                pv, carry = _sb_tile(q_ref[0, h], k, v, u_ref[...], None, car_ref[h])
                acc_ref[h] += pv
                car_ref[h] = carry
                mx = jnp.maximum(mx, jnp.max(carry[0:1, :]))
            done_ref[0] = jnp.where(mx > EXP_UNDERFLOW, 0, 1)

    @pl.when(jj == n_steps - 1)
    def _():
        for h in range(SB_HEADS):
            o_ref[0, h:h + 1, :] = acc_ref[h][0:1, :]


def _sb_decode(cache_l, page_table, q):
    b, n_pages = page_table.shape
    assert n_pages % SB_PAGES_PER_STEP == 0
    n_steps = n_pages // SB_PAGES_PER_STEP
    rows = cache_l.shape[1]
    qr = q.shape[2]

    def page_spec(i):
        return pl.BlockSpec((1, rows, SB_HEAD_DIM),
                            lambda s, j, pt: (pt[s, n_pages - 1 - (j * SB_PAGES_PER_STEP + i)], 0, 0))

    in_specs = [pl.BlockSpec((1, SB_HEADS, qr, SB_HEAD_DIM), lambda s, j, pt: (s, 0, 0, 0)),
                pl.BlockSpec((2 * KEY_TILE, 2 * KEY_TILE), lambda s, j, pt: (0, 0))]
    in_specs += [page_spec(i) for i in range(SB_PAGES_PER_STEP)]
    return pl.pallas_call(
        functools.partial(_sb_decode_kernel, n_steps=n_steps),
        grid_spec=pltpu.PrefetchScalarGridSpec(
            num_scalar_prefetch=1, grid=(b, n_steps), in_specs=in_specs,
            out_specs=pl.BlockSpec((1, SB_HEADS, SB_HEAD_DIM), lambda s, j, pt: (s, 0, 0)),
            scratch_shapes=[pltpu.VMEM((SB_HEADS, qr, SB_HEAD_DIM), F32), pltpu.VMEM((SB_HEADS, qr, LANES), F32),
                            pltpu.SMEM((1,), jnp.int32)]),
        out_shape=jax.ShapeDtypeStruct((b, SB_HEADS, SB_HEAD_DIM), F32),
        compiler_params=_cparams(("arbitrary", "arbitrary")),
        name="sb_decode",
    )(page_table, q, _suffix_matrix(), *([cache_l] * SB_PAGES_PER_STEP))


def _finish_kernel(oa_ref, ob_ref, gma_ref, gmb_ref, x_ref, wa_ref, wb_ref, wo_ref, g1_ref, b1_ref,
                   wr_ref, br_ref, h_ref, comb_ref, *, alpha, n_experts):
    ua = _dot(oa_ref[...], wa_ref[...])
    ub = _dot(ob_ref[...], wb_ref[...])
    mixed = jax.nn.sigmoid(gma_ref[...]) * ua + jax.nn.sigmoid(gmb_ref[...]) * ub
    mo = _dot(mixed.astype(BF16), wo_ref[...])
    h = _layer_norm(alpha * x_ref[...] + mo, g1_ref[...], b1_ref[...])
    h_ref[...] = h
    logits = _dot(h.astype(BF16), wr_ref[...]) + br_ref[...]
    lane = lax.broadcasted_iota(jnp.int32, logits.shape, 1)
    lane_f = lane.astype(F32)
    sc = jnp.where(lane < n_experts, logits, NEG_INF)
    vals, picks = [], []
    for _ in range(TOP_K):
        mx = jnp.max(sc, axis=1, keepdims=True)
        first = jnp.min(jnp.where(sc == mx, lane_f, 1e9), axis=1, keepdims=True)
        pick = lane_f == first
        vals.append(mx)
        picks.append(pick)
        sc = jnp.where(pick, 2.0 * NEG_INF, sc)
    es = [jnp.exp(v - vals[0]) for v in vals]
    den = es[0] + es[1] + es[2] + es[3]
    comb = jnp.zeros(logits.shape, F32)
    for k in range(TOP_K):
        comb = jnp.where(picks[k], es[k] / den, comb)
    comb_ref[...] = comb


def _finish(o_a, o_b, y, x, weights, *, tm, d_model, alpha, n_experts):
    wa, wb, wo, g1, b1, wr, br = weights
    m = x.shape[0]
    full = lambda a: pl.BlockSpec(a.shape, lambda i: (0, 0), pipeline_mode=pl.Buffered(1))
    return pl.pallas_call(
        functools.partial(_finish_kernel, alpha=alpha, n_experts=n_experts),
        grid=(m // tm,),
        in_specs=[pl.BlockSpec((tm, NSA_WIDTH), lambda i: (i, 0)),
                  pl.BlockSpec((tm, SB_WIDTH), lambda i: (i, 0)),
                  pl.BlockSpec((tm, d_model), lambda i: (i, 0)),
                  pl.BlockSpec((tm, d_model), lambda i: (i, 1)),
                  pl.BlockSpec((tm, d_model), lambda i: (i, 0)),
                  full(wa), full(wb), full(wo), full(g1), full(b1), full(wr), full(br)],
        out_specs=[pl.BlockSpec((tm, d_model), lambda i: (i, 0)),
                   pl.BlockSpec((tm, LANES), lambda i: (i, 0))],
        out_shape=[jax.ShapeDtypeStruct((m, d_model), F32), jax.ShapeDtypeStruct((m, LANES), F32)],
        compiler_params=_cparams(("arbitrary",)),
        name="out_proj_ln_route",
    )(o_a, o_b, y, y, x, wa, wb, wo, g1, b1, wr, br)


def _split_gate_up_kernel(w_ref, pg_ref, pu_ref, g_ref, u_ref):
    w = w_ref[0].astype(BF16)
    g_ref[0] = _dot(w, pg_ref[...]).astype(BF16)
    u_ref[0] = _dot(w, pu_ref[...]).astype(BF16)


def _split_gate_up(w_gate_up_l):
    e, d, f2 = w_gate_up_l.shape
    tk = min(512, d)
    pg = np.zeros((f2, f2 // 2), np.float32)
    pu = np.zeros((f2, f2 // 2), np.float32)
    pg[2 * np.arange(f2 // 2), np.arange(f2 // 2)] = 1.0
    pu[2 * np.arange(f2 // 2) + 1, np.arange(f2 // 2)] = 1.0
    out = jax.ShapeDtypeStruct((e, d, f2 // 2), BF16)
    return pl.pallas_call(
        _split_gate_up_kernel,
        grid=(e, d // tk),
        in_specs=[pl.BlockSpec((1, tk, f2), lambda i, k: (i, k, 0)),
                  pl.BlockSpec(pg.shape, lambda i, k: (0, 0)),
                  pl.BlockSpec(pu.shape, lambda i, k: (0, 0))],
        out_specs=[pl.BlockSpec((1, tk, f2 // 2), lambda i, k: (i, k, 0))] * 2,
        out_shape=[out, out],
        compiler_params=_cparams(("arbitrary", "arbitrary")),
        name="moe_split_gate_up",
    )(w_gate_up_l, jnp.asarray(pg, dtype=BF16), jnp.asarray(pu, dtype=BF16))


def _moe_kernel(h_ref, comb_ref, wg_ref, wu_ref, bg_ref, bu_ref, wd_ref, bd_ref, g2_ref, b2_ref, o_ref,
                hb_ref, acc_ref, *, alpha, n_experts):
    e = pl.program_id(1)

    @pl.when(e == 0)
    def _():
        hb_ref[...] = h_ref[...].astype(BF16)
        acc_ref[...] = jnp.zeros_like(acc_ref)

    hb = hb_ref[...]
    gate = jnp.minimum(_dot(hb, wg_ref[0]) + bg_ref[0], SWIGLU_LIMIT)
    up = jnp.clip(_dot(hb, wu_ref[0]) + bu_ref[0], -SWIGLU_LIMIT, SWIGLU_LIMIT)
    act = (up + 1.0) * gate * jax.nn.sigmoid(SWIGLU_ALPHA * gate)
    y = _dot(act.astype(BF16), wd_ref[0]) + bd_ref[0]
    lane = lax.broadcasted_iota(jnp.int32, comb_ref.shape, 1)
    c = jnp.sum(jnp.where(lane == e, comb_ref[...], 0.0), axis=1, keepdims=True)
    acc_ref[...] += c * y

    @pl.when(e == n_experts - 1)
    def _():
        o_ref[...] = _layer_norm(alpha * h_ref[...] + acc_ref[...], g2_ref[...], b2_ref[...])


def _moe(h, comb, weights, *, tm, alpha):
    wg, wu, bg, bu, wd, bd, g2, b2 = weights
    m, d_model = h.shape
    n_experts, _, d_ff = wg.shape
    return pl.pallas_call(
        functools.partial(_moe_kernel, alpha=alpha, n_experts=n_experts),
        grid=(m // tm, n_experts),
        in_specs=[pl.BlockSpec((tm, d_model), lambda i, e: (i, 0)),
                  pl.BlockSpec((tm, LANES), lambda i, e: (i, 0)),
                  pl.BlockSpec((1, d_model, d_ff), lambda i, e: (e, 0, 0)),
                  pl.BlockSpec((1, d_model, d_ff), lambda i, e: (e, 0, 0)),
                  pl.BlockSpec((1, 1, d_ff), lambda i, e: (e, 0, 0)),
                  pl.BlockSpec((1, 1, d_ff), lambda i, e: (e, 0, 0)),
                  pl.BlockSpec((1, d_ff, d_model), lambda i, e: (e, 0, 0)),
                  pl.BlockSpec((1, 1, d_model), lambda i, e: (e, 0, 0)),
                  pl.BlockSpec((1, d_model), lambda i, e: (0, 0)),
                  pl.BlockSpec((1, d_model), lambda i, e: (0, 0))],
        out_specs=pl.BlockSpec((tm, d_model), lambda i, e: (i, 0)),
        out_shape=jax.ShapeDtypeStruct((m, d_model), F32),
        scratch_shapes=[pltpu.VMEM((tm, d_model), BF16), pltpu.VMEM((tm, d_model), F32)],
        compiler_params=_cparams(("arbitrary", "arbitrary")),
        name="moe_ln",
    )(h, comb, wg, wu, bg, bu, wd, bd, g2, b2)


def _gather_cmp_kernel(pt_ref, *refs, n_steps):
    page_refs = refs[:PAGES_PER_STEP]
    new_ref, o_ref, rows_ref = refs[PAGES_PER_STEP:]
    j = pl.program_id(1)

    n_lane_tiles = NSA_KV_WIDTH // LANES

    @pl.when(j < n_steps)
    def _():
        for i in range(PAGES_PER_STEP):
            x = page_refs[i][0].T
            for c in range(n_lane_tiles):
                rows_ref[c, i * PAGE_SIZE:(i + 1) * PAGE_SIZE, :] = x[:, c * LANES:(c + 1) * LANES]

    @pl.when(j == n_steps)
    def _():
        rows_ref[...] = jnp.zeros(rows_ref.shape, F32)
        for c in range(n_lane_tiles):
            rows_ref[c, 0:8, :] = new_ref[0, :, c * LANES:(c + 1) * LANES]

    n_chunks = PAGES_PER_STEP * PAGE_SIZE // CMP_STRIDE
    for p in range(CMP_STRIDE):
        for c in range(n_lane_tiles):
            o_ref[0, :, p * NSA_KV_WIDTH + c * LANES:p * NSA_KV_WIDTH + (c + 1) * LANES] = rows_ref[
                c, pl.ds(p, n_chunks, stride=CMP_STRIDE), :].astype(BF16)


def _gather_sel_kernel(pt_ref, *refs, n_steps):
    page_refs = refs[:PAGES_PER_STEP]
    newk_ref, newv_ref, ka_ref, vot_ref = refs[PAGES_PER_STEP:]
    j = pl.program_id(1)
    tag_row = lax.broadcasted_iota(jnp.int32, (HEAD_DIM, PAGE_SIZE), 0)
    tag_col = jnp.right_shift(lax.broadcasted_iota(jnp.int32, (HEAD_DIM, PAGE_SIZE), 1), 6)
    pages_per_tile = SEL_TILE // PAGE_SIZE

    @pl.when(j < n_steps)
    def _():
        for i in range(PAGES_PER_STEP):
            xt = page_refs[i][0]
            tag = jnp.where(tag_row == (PAGE_SIZE // SEL_BLOCK) * (i % pages_per_tile) + tag_col, 1.0, 0.0)
            rs = slice(i * PAGE_SIZE, (i + 1) * PAGE_SIZE)
            for g in range(NSA_GROUPS):
                kt = xt[g * HEAD_DIM:(g + 1) * HEAD_DIM, :]
                vt = xt[(NSA_GROUPS + g) * HEAD_DIM:(NSA_GROUPS + g + 1) * HEAD_DIM, :]
                ka_ref[0, rs, g * LANES:(g + 1) * LANES] = jnp.concatenate([kt, tag], axis=0).T.astype(BF16)
                vot_ref[0, g * LANES:g * LANES + HEAD_DIM, rs] = vt.astype(BF16)
                vot_ref[0, g * LANES + HEAD_DIM:(g + 1) * LANES, rs] = jnp.ones((HEAD_DIM, PAGE_SIZE), BF16)

    @pl.when(j == n_steps)
    def _():
        ka_ref[...] = jnp.zeros(ka_ref.shape, BF16)
        vot_ref[...] = jnp.zeros(vot_ref.shape, BF16)
        ka_ref[0, 0:16, :] = newk_ref[0]
        vnew = newv_ref[0].astype(F32)
        vnew = jnp.concatenate([vnew, jnp.zeros((LANES - vnew.shape[0], NSA_PACK_WIDTH), F32)], axis=0)
        vot_ref[0, :, 0:LANES] = vnew.T.astype(BF16)


def _gather_nsa(cache_t, page_table, new_blocks, kernel_fn, outs, scratch, name):
    b, n_pages = page_table.shape
    assert n_pages % PAGES_PER_STEP == 0
    n_steps = n_pages // PAGES_PER_STEP

    def page_spec(i):
        return pl.BlockSpec((1, NSA_KV_WIDTH, PAGE_SIZE),
                            lambda s, j, pt: (pt[s, jnp.minimum(j * PAGES_PER_STEP + i, n_pages - 1)], 0, 0))

    def out_spec(shape, axis):
        return pl.BlockSpec((1,) + shape, lambda s, j, pt: (s, j, 0) if axis == 0 else (s, 0, j))

    def out_shape(shape, axis):
        full = list(shape)
        full[axis] *= n_steps + 1
        return jax.ShapeDtypeStruct((b,) + tuple(full), BF16)

    in_specs = [page_spec(i) for i in range(PAGES_PER_STEP)]
    in_specs += [pl.BlockSpec((1,) + nb.shape[1:], lambda s, j, pt: (s, 0, 0)) for nb in new_blocks]
    return pl.pallas_call(
        functools.partial(kernel_fn, n_steps=n_steps),
        grid_spec=pltpu.PrefetchScalarGridSpec(
            num_scalar_prefetch=1, grid=(b, n_steps + 1), in_specs=in_specs,
            out_specs=[out_spec(s, a) for s, a in outs], scratch_shapes=scratch),
        out_shape=[out_shape(s, a) for s, a in outs],
        compiler_params=_cparams(("arbitrary", "arbitrary")),
        name=name,
    )(page_table, *([cache_t] * PAGES_PER_STEP), *new_blocks)


def _nsa_cache_pages(cache_l):
    n_pool = cache_l.shape[0]
    return cache_l.transpose(0, 2, 3, 4, 1).reshape(n_pool, NSA_KV_WIDTH, PAGE_SIZE)


def _nsa_gates(y, off):
    b, t, _ = y.shape
    g = y[:, :, off + _OFF_GA:off + _OFF_GA + 3 * NSA_HEADS].reshape(b, t, 3, NSA_GROUPS, NSA_REP)
    g = g.transpose(0, 3, 1, 2, 4).reshape(b, NSA_GROUPS, t, 3 * NSA_REP)
    return jnp.pad(g, ((0, 0), (0, 0), (0, 0), (0, LANES - 3 * NSA_REP)))


def kernel(x_prompt, x_sample, cache_cmp_kv, cache_sel_kv, cache_sb_kv, state_win_kv, page_table, rel_bias,
           w_in, cmp_pe, cmp_w1, cmp_w2, w_up_nsa, w_up_sb, w_out, ln1_g, ln1_b,
           w_router, b_router, w_gate_up, b_gate_up, w_down, b_down, ln2_g, ln2_b):
    depth, d_model = w_in.shape[0], w_in.shape[1]
    bp, seq, _ = x_prompt.shape
    bs, dec_seq, _ = x_sample.shape
    assert bp == 1 and dec_seq == 1
    n_pages = page_table.shape[1]
    past = n_pages * PAGE_SIZE
    n_buf = state_win_kv.shape[2]
    n_experts = w_router.shape[2]
    n_pool = cache_cmp_kv.shape[1]
    alpha = (2 * depth) ** 0.25
    off = 2 * d_model
    tq_s = 16
    assert seq % SEL_TILE == 0 and past % SEL_TILE == 0

    hp = x_prompt.reshape(seq, d_model)
    hs = x_sample.reshape(bs, d_model)
    outs = [[] for _ in range(8)]
    for l in range(depth):
        w_proj = _proj_weight(w_in[l], d_model)
        cmp_w = _compress_weights(cmp_pe[l], cmp_w1[l], cmp_w2[l])
        fin_w = (w_up_nsa[l].astype(BF16), w_up_sb[l].astype(BF16), w_out[l].astype(BF16),
                 ln1_g[l][None], ln1_b[l][None],
                 jnp.pad(w_router[l], ((0, 0), (0, LANES - n_experts))).astype(BF16),
                 jnp.pad(b_router[l], (0, LANES - n_experts))[None])
        w_gate, w_upp = _split_gate_up(w_gate_up[l])
        moe_w = (w_gate, w_upp, b_gate_up[l][:, None, 0::2], b_gate_up[l][:, None, 1::2],
                 w_down[l].astype(BF16), b_down[l][:, None, :], ln2_g[l][None], ln2_b[l][None])

        tm = min(512, seq)
        y, yb = _in_proj(hp, w_proj, tm, PROJ_TN)
        kv_c = y[:, off + _OFF_KVC:off + _OFF_KVC + NSA_KV_WIDTH]
        kv_s = y[:, off + _OFF_KVS:off + _OFF_KVS + NSA_KV_WIDTH]
        kv_w = y[:, off + _OFF_KVW:off + _OFF_KVW + NSA_KV_WIDTH]
        kv_b = y[:, off + _OFF_KVB:off + _OFF_KVB + 2 * SB_WIDTH]
        q_pk, ka, vot, wk, wvt = _pack(yb, off, tm, 0, 1, True)
        n_chunks = seq // CMP_STRIDE
        chunks = yb[:, off + _OFF_KVC:off + _OFF_KVC + NSA_KV_WIDTH].reshape(1, n_chunks, CHUNK_WIDTH)
        cmpkv = _compress(chunks, seq, cmp_w)
        o_a = _nsa_attention(rel_bias, q_pk[None], _nsa_gates(y[None], off), cmpkv, ka[None], vot[None],
                             wk[None], wvt[None], n_real=KEY_TILE, q_pos0=0, win_pos0=0, n_sel_rows=seq,
                             skip_empty=False)
        o_b = _sb_attention(yb[None], off, tq=SB_PROMPT_TQ)
        h1, comb = _finish(o_a[0], o_b[0], y, hp, fin_w, tm=min(256, seq), d_model=d_model, alpha=alpha,
                           n_experts=n_experts)
        hp = _moe(h1, comb, moe_w, tm=min(512, seq), alpha=alpha)
        n_win = min(WINDOW, seq)
        outs[0].append(kv_c.reshape(1, seq, 2, NSA_GROUPS, HEAD_DIM))
        outs[1].append(kv_s.reshape(1, seq, 2, NSA_GROUPS, HEAD_DIM))
        outs[2].append(kv_b.reshape(1, seq, 2, SB_HEADS, SB_HEAD_DIM))
        outs[3].append(kv_w[seq - n_win:].reshape(1, n_win, 2, NSA_GROUPS, HEAD_DIM))

        ys, ysb = _in_proj(hs, w_proj, bs, PROJ_TN)
        kv_c = ys[:, off + _OFF_KVC:off + _OFF_KVC + NSA_KV_WIDTH]
        kv_s = ys[:, off + _OFF_KVS:off + _OFF_KVS + NSA_KV_WIDTH]
        kv_w = ys[:, off + _OFF_KVW:off + _OFF_KVW + NSA_KV_WIDTH]
        kv_b = ys[:, off + _OFF_KVB:off + _OFF_KVB + 2 * SB_WIDTH]
        q_pk, ka_new, vo_new, _, _ = _pack(ysb, off, bs, past, 0, False)
        rows_step = PAGES_PER_STEP * PAGE_SIZE
        (chunks,) = _gather_nsa(_nsa_cache_pages(cache_cmp_kv[l]), page_table,
                                [_pad_rows(kv_c[:, None, :], 1, 8)], _gather_cmp_kernel,
                                [((rows_step // CMP_STRIDE, CHUNK_WIDTH), 0)],
                                [pltpu.VMEM((NSA_KV_WIDTH // LANES, rows_step, LANES), F32)], "gather_cmp")
        ka, vot = _gather_nsa(_nsa_cache_pages(cache_sel_kv[l]), page_table,
                              [_pad_rows(ka_new[:, None, :], 1, 16), _pad_rows(vo_new[:, None, :], 1, 16)],
                              _gather_sel_kernel, [((rows_step, NSA_PACK_WIDTH), 0), ((NSA_PACK_WIDTH, rows_step), 1)],
                              [], "gather_sel")
        cmpkv = _compress(chunks, past + 1, cmp_w)
        win_all = jnp.concatenate([state_win_kv[l].reshape(bs, n_buf, NSA_KV_WIDTH), kv_w[:, None, :]], axis=1)
        lw = _round_up(n_buf + 1, KEY_TILE)
        win5 = win_all.reshape(bs, n_buf + 1, 2, NSA_GROUPS, HEAD_DIM).astype(BF16)
        wk = jnp.pad(win5[:, :, 0], ((0, 0), (0, lw - n_buf - 1), (0, 0), (0, LANES - HEAD_DIM)))
        wk = wk.reshape(bs, lw, NSA_PACK_WIDTH)
        wvt = jnp.concatenate([win5[:, :, 1], jnp.ones_like(win5[:, :, 1])], axis=-1)
        wvt = _pad_rows(wvt.transpose(0, 2, 3, 1).reshape(bs, NSA_PACK_WIDTH, n_buf + 1), 2, lw)
        o_a = _nsa_attention(rel_bias, _pad_rows(q_pk[:, None, :], 1, KEY_TILE),
                             _pad_rows(_nsa_gates(ys[:, None, :], off), 2, KEY_TILE),
                             cmpkv, ka, vot, wk, wvt, n_real=1, q_pos0=past, win_pos0=past - n_buf,
                             n_sel_rows=past + 1, skip_empty=True)
        q_b = ysb[:, off + _OFF_QB:off + _OFF_QB + SB_WIDTH].reshape(bs, SB_HEADS, 1, SB_HEAD_DIM)
        o_b = _sb_decode(cache_sb_kv[l].reshape(n_pool, PAGE_SIZE * 2 * SB_HEADS, SB_HEAD_DIM), page_table,
                         _pad_rows(q_b, 2, tq_s))
        h1, comb = _finish(o_a[:, 0], o_b.reshape(bs, SB_WIDTH).astype(BF16), ys, hs, fin_w, tm=bs,
                           d_model=d_model, alpha=alpha, n_experts=n_experts)
        hs = _moe(h1, comb, moe_w, tm=bs, alpha=alpha)
        outs[4].append(kv_c.reshape(bs, 1, 2, NSA_GROUPS, HEAD_DIM))
        outs[5].append(kv_s.reshape(bs, 1, 2, NSA_GROUPS, HEAD_DIM))
        outs[6].append(kv_b.reshape(bs, 1, 2, SB_HEADS, SB_HEAD_DIM))
        outs[7].append(win_all[:, 1:].reshape(bs, n_buf, 2, NSA_GROUPS, HEAD_DIM))

    return (hp.reshape(bp, seq, d_model), hs.reshape(bs, dec_seq, d_model)) + tuple(jnp.stack(o) for o in outs)
```

```python
import functools
import math

import numpy as np
import jax
import jax.numpy as jnp
from jax import lax
from jax.experimental import pallas as pl
from jax.experimental.pallas import tpu as pltpu

F32 = jnp.float32
BF16 = jnp.bfloat16

NSA_HEADS = 16
NSA_GROUPS = 4
NSA_REP = NSA_HEADS // NSA_GROUPS
HEAD_DIM = 64
CMP_BLOCK = 32
CMP_STRIDE = 16
CMP_HIDDEN = 64
SEL_BLOCK = 64
SEL_TOPN = 16
WINDOW = 512
SB_HEADS = 8
SB_HEAD_DIM = 128
REL_BUCKETS = 32
REL_MAX_DIST = 4096
TOP_K = 4
SWIGLU_LIMIT = 7.0
SWIGLU_ALPHA = 1.702
LN_EPS = 1e-5
NEG_INF = -1e30
FORCED_SCORE = 1e4
PAGE_SIZE = 128

LANES = 128
KEY_TILE = 128
SEL_TILE = 512
SEL_TILE_BLOCKS = SEL_TILE // SEL_BLOCK
NSA_WIDTH = NSA_HEADS * HEAD_DIM
SB_WIDTH = SB_HEADS * SB_HEAD_DIM
NSA_KV_WIDTH = 2 * NSA_GROUPS * HEAD_DIM
NSA_PACK_WIDTH = NSA_GROUPS * LANES
CHUNK_WIDTH = CMP_STRIDE * NSA_KV_WIDTH
CMP_PRE_WIDTH = 2 * NSA_GROUPS * CMP_HIDDEN
VMEM_LIMIT = 56 * 1024 * 1024
PAGES_PER_STEP = 8
SB_PROMPT_TQ = 256
EXP_UNDERFLOW = -104.0

_MAX_EXACT = REL_BUCKETS // 2
_T5_THRESH = tuple(int(math.ceil(_MAX_EXACT * 2.0 ** (j / 2.0) - 1e-9)) for j in range(1, REL_BUCKETS - _MAX_EXACT))
_FAR_DIST = _T5_THRESH[-1]
N_BIAS_TILES = -(-(_FAR_DIST + KEY_TILE) // KEY_TILE) + 1
WIN_TILES = WINDOW // KEY_TILE + 1
CMP_TAB_M_MAX = (_FAR_DIST + CMP_BLOCK - 1 + CMP_STRIDE * (LANES - 1) - 1) // KEY_TILE
CMP_TAB_X0 = -(-(KEY_TILE // CMP_STRIDE) * CMP_TAB_M_MAX // LANES) * LANES
CMP_TAB_WIDTH = CMP_TAB_X0 + 2 * LANES

_OFF_QA = 0
_OFF_KVC = _OFF_QA + NSA_WIDTH
_OFF_KVS = _OFF_KVC + NSA_KV_WIDTH
_OFF_KVW = _OFF_KVS + NSA_KV_WIDTH
_OFF_GA = _OFF_KVW + NSA_KV_WIDTH
_OFF_QB = _OFF_GA + LANES
_OFF_KVB = _OFF_QB + SB_WIDTH
_OFF_END = _OFF_KVB + 2 * SB_WIDTH
PROJ_TN = 768
PROJ_TM = 1024


def _cparams(sem):
    return pltpu.CompilerParams(dimension_semantics=sem, vmem_limit_bytes=VMEM_LIMIT)


def _round_up(x, m):
    return -(-x // m) * m


def _dot(a, b):
    return jnp.dot(a, b, preferred_element_type=F32)


def _dot_nt(a, b):
    return lax.dot_general(a, b, (((1,), (1,)), ((), ())), preferred_element_type=F32)


def _layer_norm(x, g, b):
    mu = jnp.mean(x, axis=-1, keepdims=True)
    xc = x - mu
    var = jnp.mean(xc * xc, axis=-1, keepdims=True)
    return xc * lax.rsqrt(var + LN_EPS) * g + b


def _pad_rows(x, axis, n):
    pad = [(0, 0)] * x.ndim
    pad[axis] = (0, n - x.shape[axis])
    return jnp.pad(x, pad)


def _matmul_kernel(x_ref, w_ref, o_ref, ob_ref, xb_ref):
    @pl.when(pl.program_id(1) == 0)
    def _():
        xb_ref[...] = x_ref[...].astype(BF16)

    acc = _dot(xb_ref[...], w_ref[...])
    o_ref[...] = acc
    ob_ref[...] = acc.astype(BF16)


def _in_proj(x, w, tm, tn):
    m, k = x.shape
    n = w.shape[1]
    return pl.pallas_call(
        _matmul_kernel,
        grid=(m // tm, n // tn),
        in_specs=[pl.BlockSpec((tm, k), lambda i, j: (i, 0)),
                  pl.BlockSpec((k, tn), lambda i, j: (0, j))],
        out_specs=[pl.BlockSpec((tm, tn), lambda i, j: (i, j)),
                   pl.BlockSpec((tm, tn), lambda i, j: (i, j))],
        out_shape=[jax.ShapeDtypeStruct((m, n), F32), jax.ShapeDtypeStruct((m, n), BF16)],
        scratch_shapes=[pltpu.VMEM((tm, k), BF16)],
        compiler_params=_cparams(("arbitrary", "arbitrary")),
        name="in_proj",
    )(x, w)


def _proj_weight(w_in_l, d_model):
    o = np.cumsum((0, NSA_WIDTH, NSA_KV_WIDTH, NSA_KV_WIDTH, NSA_KV_WIDTH, 3 * NSA_HEADS,
                   SB_WIDTH, 2 * SB_WIDTH, 2 * d_model))
    g_m = w_in_l[:, o[7]:o[8]]
    head = w_in_l[:, o[0]:o[5]]
    tail = w_in_l[:, o[5]:o[7]]
    pad_a = jnp.zeros((d_model, LANES - 3 * NSA_HEADS), w_in_l.dtype)
    n = 2 * d_model + _OFF_END
    n_pad = _round_up(n, PROJ_TN)
    pad_b = jnp.zeros((d_model, n_pad - n), w_in_l.dtype)
    return jnp.concatenate([g_m, head, pad_a, tail, pad_b], axis=1).astype(BF16)


def _pack_kernel(qa_ref, qb_ref, ks_ref, kw_ref, pq_ref, pk_ref, pv_ref,
                 q_ref, ka_ref, vo_ref, wk_ref, wvo_ref, *, tm, pos0, pos_stride, transpose_vo):
    q_ref[:, 0:NSA_HEADS * LANES // 2] = _dot(qa_ref[...], pq_ref[...]).astype(BF16)
    q_ref[:, NSA_HEADS * LANES // 2:] = _dot(qb_ref[...], pq_ref[...]).astype(BF16)
    lane = lax.broadcasted_iota(jnp.int32, (tm, NSA_PACK_WIDTH), 1) % LANES
    pos = pos0 + pos_stride * (pl.program_id(0) * tm + lax.broadcasted_iota(jnp.int32, (tm, NSA_PACK_WIDTH), 0))
    local_blk = jnp.right_shift(pos % SEL_TILE, 6)
    tag = (lane >= HEAD_DIM) & (lane - HEAD_DIM == local_blk)
    ka_ref[...] = jnp.where(tag, 1.0, _dot(ks_ref[...], pk_ref[...])).astype(BF16)
    vo = jnp.where(lane >= HEAD_DIM, 1.0, _dot(ks_ref[...], pv_ref[...]))
    vo_ref[...] = (vo.T if transpose_vo else vo).astype(BF16)
    wk_ref[...] = _dot(kw_ref[...], pk_ref[...]).astype(BF16)
    wvo = jnp.where(lane >= HEAD_DIM, 1.0, _dot(kw_ref[...], pv_ref[...]))
    wvo_ref[...] = (wvo.T if transpose_vo else wvo).astype(BF16)


def _pack_matrices():
    half = NSA_WIDTH // 2
    pq = np.zeros((half, half * 2), np.float32)
    for h in range(NSA_HEADS // 2):
        for d in range(HEAD_DIM):
            pq[h * HEAD_DIM + d, h * LANES + d] = HEAD_DIM ** -0.5
    pk = np.zeros((NSA_KV_WIDTH, NSA_PACK_WIDTH), np.float32)
    pv = np.zeros((NSA_KV_WIDTH, NSA_PACK_WIDTH), np.float32)
    for g in range(NSA_GROUPS):
        for d in range(HEAD_DIM):
            pk[g * HEAD_DIM + d, g * LANES + d] = 1.0
            pv[NSA_GROUPS * HEAD_DIM + g * HEAD_DIM + d, g * LANES + d] = 1.0
    return tuple(jnp.asarray(a, dtype=BF16) for a in (pq, pk, pv))


def _pack(yb, off, tm, pos0, pos_stride, transpose_vo):
    m = yb.shape[0]
    assert off % NSA_KV_WIDTH == 0 and SEL_TILE_BLOCKS <= LANES - HEAD_DIM
    c0 = off // NSA_KV_WIDTH
    mats = _pack_matrices()
    blk = lambda c: pl.BlockSpec((tm, NSA_KV_WIDTH), lambda i, c=c: (i, c))
    full = lambda a: pl.BlockSpec(a.shape, lambda i: (0, 0))
    rows_spec = lambda w: pl.BlockSpec((tm, w), lambda i: (i, 0))
    rows_shape = lambda w: jax.ShapeDtypeStruct((m, w), BF16)
    vo_spec = pl.BlockSpec((NSA_PACK_WIDTH, tm), lambda i: (0, i)) if transpose_vo else rows_spec(NSA_PACK_WIDTH)
    vo_shape = jax.ShapeDtypeStruct((NSA_PACK_WIDTH, m), BF16) if transpose_vo else rows_shape(NSA_PACK_WIDTH)
    return pl.pallas_call(
        functools.partial(_pack_kernel, tm=tm, pos0=pos0, pos_stride=pos_stride, transpose_vo=transpose_vo),
        grid=(m // tm,),
        in_specs=[blk(c0), blk(c0 + 1), blk(c0 + _OFF_KVS // NSA_KV_WIDTH), blk(c0 + _OFF_KVW // NSA_KV_WIDTH)]
        + [full(a) for a in mats],
        out_specs=[rows_spec(NSA_HEADS * LANES), rows_spec(NSA_PACK_WIDTH), vo_spec, rows_spec(NSA_PACK_WIDTH), vo_spec],
        out_shape=[rows_shape(NSA_HEADS * LANES), rows_shape(NSA_PACK_WIDTH), vo_shape, rows_shape(NSA_PACK_WIDTH),
                   vo_shape],
        compiler_params=_cparams(("arbitrary",)),
        name="nsa_pack",
    )(yb, yb, yb, yb, *mats)


def _compress_kernel(x_ref, pe_ref, w1_ref, w2_ref, o_ref, acc_ref, pacc_ref, *, nk, nb):
    kk = pl.program_id(1)

    @pl.when(kk == 0)
    def _():
        acc_ref[...] = jnp.zeros_like(acc_ref)
        pacc_ref[...] = jnp.zeros_like(pacc_ref)

    w1 = w1_ref[...]
    acc_ref[...] += _dot(x_ref[0], w1)
    pacc_ref[...] += _dot(pe_ref[...].astype(BF16), w1)

    @pl.when(kk == nk - 1)
    def _():
        h = CMP_PRE_WIDTH
        pe_bias = pacc_ref[0:1, 0:h] + pacc_ref[1:2, h:2 * h]
        pre = acc_ref[0:nb, 0:h] + acc_ref[1:nb + 1, h:2 * h] + pe_bias
        act = jax.nn.gelu(pre)
        o_ref[0] = _dot(act.astype(BF16), w2_ref[...]).astype(BF16)


def _compress_weights(cmp_pe_l, cmp_w1_l, cmp_w2_l):
    halves = CMP_BLOCK // CMP_STRIDE
    w1h = cmp_w1_l.reshape(2, halves, CMP_STRIDE, HEAD_DIM, CMP_HIDDEN)
    eye_k = jnp.eye(2, dtype=F32)
    eye_g = jnp.eye(NSA_GROUPS, dtype=F32)
    w1big = jnp.einsum('khpdf,kK,gG->pKGdhkgf', w1h, eye_k, eye_g).reshape(CHUNK_WIDTH, halves * CMP_PRE_WIDTH)
    w2big = jnp.einsum('kfd,kK,gG->kgfGKd', cmp_w2_l, eye_k, eye_g).reshape(CMP_PRE_WIDTH, NSA_PACK_WIDTH)
    pe = cmp_pe_l.reshape(2, halves, CMP_STRIDE, HEAD_DIM).transpose(1, 2, 0, 3)
    pe = jnp.broadcast_to(pe[:, :, :, None, :], (halves, CMP_STRIDE, 2, NSA_GROUPS, HEAD_DIM))
    pe_rows = jnp.pad(pe.reshape(halves, CHUNK_WIDTH), ((0, 8 - halves), (0, 0)))
    return pe_rows, w1big.astype(BF16), w2big.astype(BF16)


def _compress(chunks, n_rows, weights):
    pe_rows, w1big, w2big = weights
    b = chunks.shape[0]
    n_chunks = -(-n_rows // CMP_STRIDE)
    nb = _round_up(n_chunks - 1, LANES)
    ncp = nb + 16
    if chunks.shape[1] < ncp:
        chunks = _pad_rows(chunks, 1, ncp)
    kstep = 2048
    nk = CHUNK_WIDTH // kstep
    nw = w1big.shape[1]
    return pl.pallas_call(
        functools.partial(_compress_kernel, nk=nk, nb=nb),
        grid=(b, nk),
        in_specs=[pl.BlockSpec((1, ncp, kstep), lambda i, k: (i, 0, k)),
                  pl.BlockSpec((8, kstep), lambda i, k: (0, k)),
                  pl.BlockSpec((kstep, nw), lambda i, k: (k, 0)),
                  pl.BlockSpec(w2big.shape, lambda i, k: (0, 0))],
        out_specs=pl.BlockSpec((1, nb, NSA_PACK_WIDTH), lambda i, k: (i, 0, 0)),
        out_shape=jax.ShapeDtypeStruct((b, nb, NSA_PACK_WIDTH), BF16),
        scratch_shapes=[pltpu.VMEM((ncp, nw), F32), pltpu.VMEM((8, nw), F32)],
        compiler_params=_cparams(("arbitrary", "arbitrary")),
        name="nsa_compress",
    )(chunks, pe_rows, w1big, w2big)


def _t5_bucket(d):
    n = jnp.maximum(d, 0)
    cnt = jnp.zeros_like(n)
    for th in _T5_THRESH:
        cnt = cnt + jnp.where(n >= th, 1, 0)
    return jnp.where(n < _MAX_EXACT, n, _MAX_EXACT + cnt)


def _bias_tiles(d, rel_ref, g):
    bucket = _t5_bucket(d)
    outs = [jnp.zeros(d.shape, F32) for _ in range(NSA_REP)]
    for j in range(REL_BUCKETS):
        hit = bucket == j
        for r in range(NSA_REP):
            outs[r] = jnp.where(hit, rel_ref[j, NSA_REP * g + r], outs[r])
    return tuple(outs)


def _sel_head_step(st_ref, adjust, shift, cs, mt_ref, acct_ref, pt_ref):
    chunks = [slice(c * KEY_TILE, (c + 1) * KEY_TILE) for c in range(SEL_TILE // KEY_TILE)]
    mx = None
    for c, ks in enumerate(chunks):
        x = st_ref[ks, cs]
        if adjust is not None:
            x = adjust(c, x)
            st_ref[ks, cs] = x
        mx = x if mx is None else jnp.maximum(mx, x)
    m_old = mt_ref[:, cs]
    m_new = jnp.maximum(m_old, jnp.max(mx, axis=0, keepdims=True) + shift)
    acct_ref[:, cs] = jnp.exp(m_old - m_new) * acct_ref[:, cs]
    off = m_new - shift
    for ks in chunks:
        pt_ref[ks, cs] = jnp.exp(st_ref[ks, cs] - off).astype(BF16)
    mt_ref[:, cs] = m_new


def _nsa_kernel(rel_ref, q_ref, gate_ref, cmp_ref, ka_ref, vot_ref, *rest,
                tq, n_real, q_pos0, win_pos0, nb, nsbp, n_sb, ls, lw, skip_empty):
    wk_refs = rest[:WIN_TILES]
    wvt_refs = rest[WIN_TILES:2 * WIN_TILES]
    (ov_ref, o_ref, tabt_ref, ctab_ref, wbias_ref, s_ref, pent_ref, qat_ref, st_ref, pt_ref, st2_ref, pt2_ref,
     acct_ref, mt_ref) = rest[2 * WIN_TILES:]
    g = pl.program_id(0)
    b = pl.program_id(1)
    qt = pl.program_id(2)
    t0 = q_pos0 + qt * tq
    rows = lax.broadcasted_iota(jnp.int32, (tq, LANES), 0)
    cols = lax.broadcasted_iota(jnp.int32, (tq, LANES), 1)
    tpos = t0 + rows
    far_idx = N_BIAS_TILES - 1

    @pl.when((b == 0) & (qt == 0))
    def _():
        def body(di, c):
            outs_t = _bias_tiles(di * KEY_TILE + cols - rows, rel_ref, g)
            for r in range(NSA_REP):
                tabt_ref[di, r] = outs_t[r]
            return c
        lax.fori_loop(0, N_BIAS_TILES, body, 0)
        for i in range(WIN_TILES):
            d = (WIN_TILES - 1 - i) * KEY_TILE + cols - rows
            outs = _bias_tiles(d, rel_ref, g)
            ok = (d >= 0) & (d < WINDOW)
            for r in range(NSA_REP):
                wbias_ref[r, i * KEY_TILE:(i + 1) * KEY_TILE, :] = jnp.where(ok, outs[r], NEG_INF)
        for xt in range(CMP_TAB_WIDTH // LANES):
            d = rows - (CMP_BLOCK - 1) + CMP_STRIDE * (CMP_TAB_X0 - (xt * LANES + cols))
            outs = _bias_tiles(d, rel_ref, g)
            for r in range(NSA_REP):
                ctab_ref[r, :, xt * LANES:(xt + 1) * LANES] = outs[r]

    qs = [q_ref[0, :, r * LANES:(r + 1) * LANES] for r in range(NSA_REP)]
    q = jnp.concatenate(qs, axis=0)
    row_slices = [pl.ds(r * tq, tq) for r in range(NSA_REP)]

    for ct in range(nb // LANES):
        cs_ct = slice(ct * LANES, (ct + 1) * LANES)
        live = t0 + tq - 1 >= CMP_STRIDE * LANES * ct + CMP_BLOCK - 1

        @pl.when(live)
        def _(ct=ct, cs_ct=cs_ct):
            s = _dot_nt(q, cmp_ref[0, cs_ct, :])
            d = tpos - (CMP_BLOCK - 1) - CMP_STRIDE * (ct * LANES + cols)
            m_idx = jnp.right_shift(t0, 7) - (LANES * CMP_STRIDE // KEY_TILE) * ct

            def far_fn():
                return tuple(jnp.full((tq, LANES), rel_ref[REL_BUCKETS - 1, NSA_REP * g + r], F32)
                             for r in range(NSA_REP))

            def near_fn():
                start = CMP_TAB_X0 - (KEY_TILE // CMP_STRIDE) * m_idx
                a0 = pl.multiple_of(jnp.right_shift(start, 7) * LANES, LANES)
                shift = (2 * LANES - (start - a0)) % (2 * LANES)
                return tuple(pltpu.roll(ctab_ref[r, :, pl.ds(a0, 2 * LANES)], shift, 1)[:, :LANES]
                             for r in range(NSA_REP))

            bias = lax.cond((m_idx >= 0) & (m_idx <= CMP_TAB_M_MAX), near_fn, far_fn)
            valid = d >= 0
            for r in range(NSA_REP):
                s_ref[row_slices[r], cs_ct] = jnp.where(valid, s[r * tq:(r + 1) * tq] + bias[r], NEG_INF)

        @pl.when(jnp.logical_not(live))
        def _(cs_ct=cs_ct):
            s_ref[:, cs_ct] = jnp.full((NSA_REP * tq, LANES), NEG_INF, F32)

    s = s_ref[:, 0:nb]
    valid = s > 0.5 * NEG_INF
    p = jnp.where(valid, jnp.exp(s - jnp.max(s, axis=1, keepdims=True)), 0.0)
    pc = p / jnp.maximum(jnp.sum(p, axis=1, keepdims=True), 1e-30)
    o_cmp = _dot(pc.astype(BF16), cmp_ref[0])

    psum = pc[0:tq] + pc[tq:2 * tq] + pc[2 * tq:3 * tq] + pc[3 * tq:4 * tq]
    p_hi = psum.astype(BF16)
    p_lo = (psum - p_hi.astype(F32)).astype(BF16)
    imp = _dot(p_hi, ov_ref[...]) + _dot(p_lo, ov_ref[...])
    blk = lax.broadcasted_iota(jnp.int32, (tq, nsbp), 1)
    blk_f = blk.astype(F32)
    sel_rows = lax.broadcasted_iota(jnp.int32, (tq, nsbp), 0)
    qb = jnp.right_shift(t0 + sel_rows, 6)
    forced = (blk == 0) | (blk == qb) | (blk == qb - 1)
    score = jnp.where(forced, FORCED_SCORE, jnp.where(blk <= qb, imp, -1.0))
    score = jnp.where(blk < n_sb, score, -2.0)
    score = score.T
    blk_t = lax.broadcasted_iota(jnp.int32, (nsbp, tq), 0).astype(F32)
    sel_t = jnp.zeros((nsbp, tq), F32)
    for _ in range(min(SEL_TOPN, n_sb)):
        mx = jnp.max(score, axis=0, keepdims=True)
        first = jnp.min(jnp.where(score == mx, blk_t, 1e9), axis=0, keepdims=True)
        pick = blk_t == first
        sel_t = jnp.where(pick, 1.0, sel_t)
        score = jnp.where(pick, -3.0, score)
    if n_real < tq:
        real_q = lax.broadcasted_iota(jnp.int32, (nsbp, tq), 1) < n_real
        sel_t = jnp.where(real_q, sel_t, 0.0)
    pent_ref[...] = (sel_t - 1.0) * 1e30

    mt_ref[...] = jnp.full(mt_ref.shape, NEG_INF, F32)
    acct_ref[...] = jnp.zeros(acct_ref.shape, F32)
    qat_ref[...] = q.astype(F32).T
    col_slices = [slice(r * tq, (r + 1) * tq) for r in range(NSA_REP)]

    kb_first = (q_pos0 - win_pos0) // KEY_TILE - (WIN_TILES - 1) + qt
    st_ref[...] = _dot(jnp.concatenate([wk_refs[i][0] for i in range(WIN_TILES)], axis=0),
                       qat_ref[...].astype(BF16))
    for r in range(NSA_REP):
        parts = []
        for i in range(WIN_TILES):
            ks = slice(i * KEY_TILE, (i + 1) * KEY_TILE)
            in_range = (kb_first + i >= 0) & (kb_first + i < lw // KEY_TILE)
            parts.append(jnp.where(in_range, st_ref[ks, col_slices[r]] + wbias_ref[r, ks, :], NEG_INF))
        st = jnp.concatenate(parts, axis=0)
        pt_ref[:, col_slices[r]] = jnp.exp(st - jnp.max(st, axis=0, keepdims=True)).astype(BF16)
    acc_w = _dot(jnp.concatenate([wvt_refs[i][0] for i in range(WIN_TILES)], axis=1), pt_ref[...])
    o_win_t = acc_w[0:HEAD_DIM] / jnp.maximum(acc_w[HEAD_DIM:], 1e-30)

    key_i = lax.broadcasted_iota(jnp.int32, (KEY_TILE, tq), 0)
    qry_i = lax.broadcasted_iota(jnp.int32, (KEY_TILE, tq), 1)
    real_pen = lax.broadcasted_iota(jnp.int32, (SEL_TILE_BLOCKS, tq), 1) < n_real

    def tile_pen(kt):
        return pent_ref[pl.ds(pl.multiple_of(kt * SEL_TILE_BLOCKS, SEL_TILE_BLOCKS), SEL_TILE_BLOCKS), :]

    def tile_scores(kt, pen, slot):
        for r in range(NSA_REP):
            qat_ref[HEAD_DIM:HEAD_DIM + SEL_TILE_BLOCKS, col_slices[r]] = pen
        st2_ref[slot] = _dot(ka_ref[0, pl.ds(pl.multiple_of(kt * SEL_TILE, SEL_TILE), SEL_TILE), :],
                             qat_ref[...].astype(BF16))

    def tile_softmax(kt, slot, near):
        k0 = kt * SEL_TILE
        di0 = jnp.right_shift(t0 - k0, 7)
        for r in range(NSA_REP):
            if near:
                def adjust(c4, x, r=r):
                    bias = tabt_ref[jnp.clip(di0 - c4, 0, far_idx), r]
                    return jnp.where(k0 + c4 * KEY_TILE + key_i <= t0 + qry_i, x + bias, NEG_INF)
                shift = 0.0
            else:
                adjust = None
                shift = rel_ref[REL_BUCKETS - 1, NSA_REP * g + r]
            _sel_head_step(st2_ref.at[slot], adjust, shift, col_slices[r], mt_ref, acct_ref, pt2_ref.at[slot])

    def tile_values(kt, slot):
        acct_ref[...] += _dot(vot_ref[0, :, pl.ds(pl.multiple_of(kt * SEL_TILE, SEL_TILE), SEL_TILE)], pt2_ref[slot])

    n_kt = jnp.minimum(jnp.right_shift(t0 + tq - 1, 9) + 1, ls // SEL_TILE)
    n_far = jnp.clip(jnp.right_shift(t0 - _FAR_DIST - (SEL_TILE - 1), 9) + 1, 0, n_kt)

    if skip_empty:
        def sel_body(kt, c):
            pen = tile_pen(kt)

            @pl.when(jnp.max(jnp.where(real_pen, pen, NEG_INF)) > -1.0)
            def _():
                tile_scores(kt, pen, 0)
                pl.when(kt < n_far)(lambda: tile_softmax(kt, 0, False))
                pl.when(kt >= n_far)(lambda: tile_softmax(kt, 0, True))
                tile_values(kt, 0)
            return c
        lax.fori_loop(0, n_kt, sel_body, 0)
    else:
        def run(lo, hi, near):
            @pl.when(hi > lo)
            def _():
                pt2_ref[1] = jnp.zeros((SEL_TILE, NSA_REP * tq), BF16)
                tile_scores(lo, tile_pen(lo), 0)

                def body(p, c):
                    i = lo + 2 * p
                    nxt = jnp.minimum(i + 1, hi - 1)
                    tile_scores(nxt, tile_pen(nxt), 1)
                    tile_values(jnp.maximum(i - 1, lo), 1)
                    tile_softmax(i, 0, near)

                    @pl.when(i + 1 < hi)
                    def _():
                        nx2 = jnp.minimum(i + 2, hi - 1)
                        tile_scores(nx2, tile_pen(nx2), 0)
                        tile_values(i, 0)
                        tile_softmax(i + 1, 1, near)
                    return c
                lax.fori_loop(0, jnp.right_shift(hi - lo + 1, 1), body, 0)
                last_odd = (hi - 1 - lo) % 2 == 1
                pl.when(last_odd)(lambda: tile_values(hi - 1, 1))
                pl.when(jnp.logical_not(last_odd))(lambda: tile_values(hi - 1, 0))

        run(0, n_far, False)
        run(n_far, n_kt, True)
    acct = acct_ref[...]
    o_sel_t = acct[0:HEAD_DIM] / jnp.maximum(acct[HEAD_DIM:], 1e-30)
    o_sel = jnp.concatenate([jnp.zeros_like(o_sel_t), o_sel_t], axis=0).T
    o_win = jnp.concatenate([jnp.zeros_like(o_win_t), o_win_t], axis=0).T

    gt = jax.nn.sigmoid(gate_ref[0, 0])
    heads = []
    for r in range(NSA_REP):
        rs = slice(r * tq, (r + 1) * tq)
        heads.append(gt[:, r:r + 1] * o_cmp[rs] + gt[:, NSA_REP + r:NSA_REP + r + 1] * o_sel[rs]
                     + gt[:, 2 * NSA_REP + r:2 * NSA_REP + r + 1] * o_win[rs])
    low = cols < HEAD_DIM
    for pair in range(NSA_REP // 2):
        left = pltpu.roll(heads[2 * pair], HEAD_DIM, 1)
        o_ref[0, :, pair * LANES:(pair + 1) * LANES] = jnp.where(low, left, heads[2 * pair + 1]).astype(BF16)


def _overlap_matrix(nb, nsbp):
    cs = np.arange(nb)[:, None] * CMP_STRIDE
    ss = np.arange(nsbp)[None, :] * SEL_BLOCK
    ov = np.minimum(cs + CMP_BLOCK, ss + SEL_BLOCK) - np.maximum(cs, ss)
    return jnp.asarray(np.maximum(ov, 0).astype(np.float32) / CMP_BLOCK, dtype=BF16)


def _nsa_attention(rel_bias, q, gates, cmpkv, ka, vot, wk, wvt, *, n_real, q_pos0, win_pos0, n_sel_rows,
                   skip_empty):
    b, tp, _ = q.shape
    nb = cmpkv.shape[1]
    ls = ka.shape[1]
    lw = wk.shape[1]
    tq = LANES
    n_sb = -(-n_sel_rows // SEL_BLOCK)
    nsbp = _round_up(n_sb, LANES)
    assert q_pos0 % SEL_TILE == 0 and win_pos0 % KEY_TILE == 0 and n_sb >= SEL_TOPN
    assert ls % SEL_TILE == 0 and lw % KEY_TILE == 0 and tp % tq == 0 and vot.shape[2] == ls and wvt.shape[2] == lw
    ov = _overlap_matrix(nb, nsbp)
    kern = functools.partial(_nsa_kernel, tq=tq, n_real=n_real, q_pos0=q_pos0, win_pos0=win_pos0, nb=nb,
                             nsbp=nsbp, n_sb=n_sb, ls=ls, lw=lw, skip_empty=skip_empty)
    kb_first = (q_pos0 - win_pos0) // KEY_TILE - (WIN_TILES - 1)
    win_blk = lambda t, i: jnp.clip(kb_first + t + i, 0, lw // KEY_TILE - 1)
    wk_specs = [pl.BlockSpec((1, KEY_TILE, LANES), lambda g, s, t, i=i: (s, win_blk(t, i), g))
                for i in range(WIN_TILES)]
    wvt_specs = [pl.BlockSpec((1, LANES, KEY_TILE), lambda g, s, t, i=i: (s, g, win_blk(t, i)))
                 for i in range(WIN_TILES)]
    rows_all = NSA_REP * tq
    return pl.pallas_call(
        kern,
        grid=(NSA_GROUPS, b, tp // tq),
        in_specs=[pl.BlockSpec(memory_space=pltpu.SMEM),
                  pl.BlockSpec((1, tq, NSA_REP * LANES), lambda g, i, t: (i, t, g)),
                  pl.BlockSpec((1, 1, tq, LANES), lambda g, i, t: (i, g, t, 0)),
                  pl.BlockSpec((1, nb, LANES), lambda g, i, t: (i, 0, g)),
                  pl.BlockSpec((1, ls, LANES), lambda g, i, t: (i, 0, g)),
                  pl.BlockSpec((1, LANES, ls), lambda g, i, t: (i, g, 0))]
        + wk_specs + wvt_specs + [pl.BlockSpec((nb, nsbp), lambda g, i, t: (0, 0))],
        out_specs=pl.BlockSpec((1, tq, NSA_REP * HEAD_DIM), lambda g, i, t: (i, t, g)),
        out_shape=jax.ShapeDtypeStruct((b, tp, NSA_WIDTH), BF16),
        scratch_shapes=[pltpu.VMEM((N_BIAS_TILES, NSA_REP, KEY_TILE, tq), F32),
                        pltpu.VMEM((NSA_REP, tq, CMP_TAB_WIDTH), F32),
                        pltpu.VMEM((NSA_REP, WIN_TILES * KEY_TILE, tq), F32),
                        pltpu.VMEM((rows_all, nb), F32),
                        pltpu.VMEM((nsbp, tq), F32),
                        pltpu.VMEM((LANES, rows_all), F32),
                        pltpu.VMEM((WIN_TILES * KEY_TILE, rows_all), F32),
                        pltpu.VMEM((WIN_TILES * KEY_TILE, rows_all), BF16),
                        pltpu.VMEM((2, SEL_TILE, rows_all), F32),
                        pltpu.VMEM((2, SEL_TILE, rows_all), BF16),
                        pltpu.VMEM((LANES, rows_all), F32),
                        pltpu.VMEM((1, rows_all), F32)],
        compiler_params=_cparams(("arbitrary", "arbitrary", "arbitrary")),
        name="nsa_attention",
    )(rel_bias, q, gates, cmpkv, ka, vot, *([wk] * WIN_TILES), *([wvt] * WIN_TILES), ov)


def _sb_tile(q, k, v, u, causal, carry):
    z = _dot_nt(q, k) * (SB_HEAD_DIM ** -0.5)
    softplus = jnp.maximum(z, 0.0) + jnp.log(1.0 + jnp.exp(-jnp.abs(z)))
    log_stay = -softplus if causal is None else jnp.where(causal, -softplus, 0.0)
    hi = log_stay.astype(BF16)
    lo = (log_stay - hi.astype(F32)).astype(BF16)
    sums = _dot(jnp.concatenate([hi, lo], axis=1), u)
    a = jnp.exp((z - softplus) + sums[:, :LANES] + carry)
    if causal is not None:
        a = jnp.where(causal, a, 0.0)
    return _dot(a.astype(BF16), v), carry + sums[:, LANES:]


def _sb_kernel(q_ref, k_ref, v_ref, u_ref, o_ref, acc_ref, car_ref, *, tq, q_pos0, lp):
    qt = pl.program_id(2)
    t0 = q_pos0 + qt * tq
    q = q_ref[0]
    rows = lax.broadcasted_iota(jnp.int32, (tq, LANES), 0)
    cols = lax.broadcasted_iota(jnp.int32, (tq, LANES), 1)
    tpos = t0 + rows
    acc_ref[...] = jnp.zeros(acc_ref.shape, F32)
    car_ref[...] = jnp.zeros(car_ref.shape, F32)

    def cond(c):
        kt, go = c
        return (kt >= 0) & go

    def body(c):
        kt, _ = c
        k0 = pl.multiple_of(kt * KEY_TILE, KEY_TILE)
        pv, carry = _sb_tile(q, k_ref[0, pl.ds(k0, KEY_TILE), :], v_ref[0, pl.ds(k0, KEY_TILE), :], u_ref[...],
                             k0 + cols < tpos, car_ref[...])
        acc_ref[...] += pv
        car_ref[...] = carry
        return kt - 1, jnp.max(carry) > EXP_UNDERFLOW

    kt_hi = jnp.minimum(jnp.right_shift(t0 + tq - 2, 7), lp // KEY_TILE - 1)
    lax.while_loop(cond, body, (kt_hi, True))
    o_ref[0] = acc_ref[...].astype(BF16)


def _suffix_matrix():
    j = np.arange(2 * KEY_TILE)[:, None] % KEY_TILE
    s = np.arange(2 * KEY_TILE)[None, :]
    u = np.where(s < KEY_TILE, j > s, True)
    return jnp.asarray(u.astype(np.float32), dtype=BF16)


def _sb_attention(yb, off, *, tq):
    _, t, _ = yb.shape
    assert t % KEY_TILE == 0 and t % tq == 0 and tq >= 2
    cq = (off + _OFF_QB) // SB_HEAD_DIM
    ck = (off + _OFF_KVB) // SB_HEAD_DIM
    cv = ck + SB_HEADS
    return pl.pallas_call(
        functools.partial(_sb_kernel, tq=tq, q_pos0=0, lp=t),
        grid=(1, SB_HEADS, t // tq),
        in_specs=[pl.BlockSpec((1, tq, SB_HEAD_DIM), lambda i, h, s: (i, s, cq + h)),
                  pl.BlockSpec((1, t, SB_HEAD_DIM), lambda i, h, s: (i, 0, ck + h)),
                  pl.BlockSpec((1, t, SB_HEAD_DIM), lambda i, h, s: (i, 0, cv + h)),
                  pl.BlockSpec((2 * KEY_TILE, 2 * KEY_TILE), lambda i, h, s: (0, 0))],
        out_specs=pl.BlockSpec((1, tq, SB_HEAD_DIM), lambda i, h, s: (i, s, h)),
        out_shape=jax.ShapeDtypeStruct((1, t, SB_WIDTH), BF16),
        scratch_shapes=[pltpu.VMEM((tq, SB_HEAD_DIM), F32), pltpu.VMEM((tq, LANES), F32)],
        compiler_params=_cparams(("arbitrary", "arbitrary", "arbitrary")),
        name="sb_attention",
    )(yb, yb, yb, _suffix_matrix())


def _sb_decode_kernel(pt_ref, q_ref, u_ref, cache_ref, o_ref, buf_ref, sem_ref, acc_ref, car_ref, *, n_pages):
    s = pl.program_id(0)
    acc_ref[...] = jnp.zeros(acc_ref.shape, F32)
    car_ref[...] = jnp.zeros(car_ref.shape, F32)

    def page_copy(j, slot):
        return pltpu.make_async_copy(cache_ref.at[pt_ref[s, j]], buf_ref.at[slot], sem_ref.at[slot])

    def slot_of(j):
        return (n_pages - 1 - j) % 2

    page_copy(n_pages - 1, 0).start()

    def cond(c):
        j, go = c
        return (j >= 0) & go

    def body(c):
        j, _ = c
        slot = slot_of(j)

        @pl.when(j > 0)
        def _():
            page_copy(j - 1, 1 - slot).start()

        page_copy(j, slot).wait()
        mx = jnp.float32(2.0 * NEG_INF)
        for h in range(SB_HEADS):
            k = buf_ref[slot, pl.ds(h, PAGE_SIZE, stride=2 * SB_HEADS), :].astype(BF16)
            v = buf_ref[slot, pl.ds(SB_HEADS + h, PAGE_SIZE, stride=2 * SB_HEADS), :].astype(BF16)
            pv, carry = _sb_tile(q_ref[0, h], k, v, u_ref[...], None, car_ref[h])
            acc_ref[h] += pv
            car_ref[h] = carry
            mx = jnp.maximum(mx, jnp.max(carry[0:1, :]))
        return j - 1, mx > EXP_UNDERFLOW

    j_end, _ = lax.while_loop(cond, body, (n_pages - 1, True))

    @pl.when(j_end >= 0)
    def _():
        page_copy(j_end, slot_of(j_end)).wait()

    for h in range(SB_HEADS):
        o_ref[0, h:h + 1, :] = acc_ref[h][0:1, :]


def _sb_decode(cache_l, page_table, q):
    b, n_pages = page_table.shape
    rows = cache_l.shape[1]
    qr = q.shape[2]
    return pl.pallas_call(
        functools.partial(_sb_decode_kernel, n_pages=n_pages),
        grid_spec=pltpu.PrefetchScalarGridSpec(
            num_scalar_prefetch=1, grid=(b,),
            in_specs=[pl.BlockSpec((1, SB_HEADS, qr, SB_HEAD_DIM), lambda s, pt: (s, 0, 0, 0)),
                      pl.BlockSpec((2 * KEY_TILE, 2 * KEY_TILE), lambda s, pt: (0, 0)),
                      pl.BlockSpec(memory_space=pl.ANY)],
            out_specs=pl.BlockSpec((1, SB_HEADS, SB_HEAD_DIM), lambda s, pt: (s, 0, 0)),
            scratch_shapes=[pltpu.VMEM((2, rows, SB_HEAD_DIM), F32), pltpu.SemaphoreType.DMA((2,)),
                            pltpu.VMEM((SB_HEADS, qr, SB_HEAD_DIM), F32), pltpu.VMEM((SB_HEADS, qr, LANES), F32)]),
        out_shape=jax.ShapeDtypeStruct((b, SB_HEADS, SB_HEAD_DIM), F32),
        compiler_params=_cparams(("arbitrary",)),
        name="sb_decode",
    )(page_table, q, _suffix_matrix(), cache_l)


def _finish_kernel(oa_ref, ob_ref, gma_ref, gmb_ref, x_ref, wa_ref, wb_ref, wo_ref, g1_ref, b1_ref,
                   wr_ref, br_ref, h_ref, comb_ref, *, alpha, n_experts):
    ua = _dot(oa_ref[...], wa_ref[...])
    ub = _dot(ob_ref[...], wb_ref[...])
    mixed = jax.nn.sigmoid(gma_ref[...]) * ua + jax.nn.sigmoid(gmb_ref[...]) * ub
    mo = _dot(mixed.astype(BF16), wo_ref[...])
    h = _layer_norm(alpha * x_ref[...] + mo, g1_ref[...], b1_ref[...])
    h_ref[...] = h
    logits = _dot(h.astype(BF16), wr_ref[...]) + br_ref[...]
    lane = lax.broadcasted_iota(jnp.int32, logits.shape, 1)
    lane_f = lane.astype(F32)
    sc = jnp.where(lane < n_experts, logits, NEG_INF)
    vals, picks = [], []
    for _ in range(TOP_K):
        mx = jnp.max(sc, axis=1, keepdims=True)
        first = jnp.min(jnp.where(sc == mx, lane_f, 1e9), axis=1, keepdims=True)
        pick = lane_f == first
        vals.append(mx)
        picks.append(pick)
        sc = jnp.where(pick, 2.0 * NEG_INF, sc)
    es = [jnp.exp(v - vals[0]) for v in vals]
    den = es[0] + es[1] + es[2] + es[3]
    comb = jnp.zeros(logits.shape, F32)
    for k in range(TOP_K):
        comb = jnp.where(picks[k], es[k] / den, comb)
    comb_ref[...] = comb


def _finish(o_a, o_b, y, x, weights, *, tm, d_model, alpha, n_experts):
    wa, wb, wo, g1, b1, wr, br = weights
    m = x.shape[0]
    full = lambda a: pl.BlockSpec(a.shape, lambda i: (0, 0), pipeline_mode=pl.Buffered(1))
    return pl.pallas_call(
        functools.partial(_finish_kernel, alpha=alpha, n_experts=n_experts),
        grid=(m // tm,),
        in_specs=[pl.BlockSpec((tm, NSA_WIDTH), lambda i: (i, 0)),
                  pl.BlockSpec((tm, SB_WIDTH), lambda i: (i, 0)),
                  pl.BlockSpec((tm, d_model), lambda i: (i, 0)),
                  pl.BlockSpec((tm, d_model), lambda i: (i, 1)),
                  pl.BlockSpec((tm, d_model), lambda i: (i, 0)),
                  full(wa), full(wb), full(wo), full(g1), full(b1), full(wr), full(br)],
        out_specs=[pl.BlockSpec((tm, d_model), lambda i: (i, 0)),
                   pl.BlockSpec((tm, LANES), lambda i: (i, 0))],
        out_shape=[jax.ShapeDtypeStruct((m, d_model), F32), jax.ShapeDtypeStruct((m, LANES), F32)],
        compiler_params=_cparams(("arbitrary",)),
        name="out_proj_ln_route",
    )(o_a, o_b, y, y, x, wa, wb, wo, g1, b1, wr, br)


def _split_gate_up_kernel(w_ref, pg_ref, pu_ref, g_ref, u_ref):
    w = w_ref[0].astype(BF16)
    g_ref[0] = _dot(w, pg_ref[...]).astype(BF16)
    u_ref[0] = _dot(w, pu_ref[...]).astype(BF16)


def _split_gate_up(w_gate_up_l):
    e, d, f2 = w_gate_up_l.shape
    tk = min(512, d)
    pg = np.zeros((f2, f2 // 2), np.float32)
    pu = np.zeros((f2, f2 // 2), np.float32)
    pg[2 * np.arange(f2 // 2), np.arange(f2 // 2)] = 1.0
    pu[2 * np.arange(f2 // 2) + 1, np.arange(f2 // 2)] = 1.0
    out = jax.ShapeDtypeStruct((e, d, f2 // 2), BF16)
    return pl.pallas_call(
        _split_gate_up_kernel,
        grid=(e, d // tk),
        in_specs=[pl.BlockSpec((1, tk, f2), lambda i, k: (i, k, 0)),
                  pl.BlockSpec(pg.shape, lambda i, k: (0, 0)),
                  pl.BlockSpec(pu.shape, lambda i, k: (0, 0))],
        out_specs=[pl.BlockSpec((1, tk, f2 // 2), lambda i, k: (i, k, 0))] * 2,
        out_shape=[out, out],
        compiler_params=_cparams(("arbitrary", "arbitrary")),
        name="moe_split_gate_up",
    )(w_gate_up_l, jnp.asarray(pg, dtype=BF16), jnp.asarray(pu, dtype=BF16))


def _moe_kernel(h_ref, comb_ref, wg_ref, wu_ref, bg_ref, bu_ref, wd_ref, bd_ref, g2_ref, b2_ref, o_ref,
                hb_ref, acc_ref, *, alpha, n_experts):
    e = pl.program_id(1)

    @pl.when(e == 0)
    def _():
        hb_ref[...] = h_ref[...].astype(BF16)
        acc_ref[...] = jnp.zeros_like(acc_ref)

    hb = hb_ref[...]
    gate = jnp.minimum(_dot(hb, wg_ref[0]) + bg_ref[0], SWIGLU_LIMIT)
    up = jnp.clip(_dot(hb, wu_ref[0]) + bu_ref[0], -SWIGLU_LIMIT, SWIGLU_LIMIT)
    act = (up + 1.0) * gate * jax.nn.sigmoid(SWIGLU_ALPHA * gate)
    y = _dot(act.astype(BF16), wd_ref[0]) + bd_ref[0]
    lane = lax.broadcasted_iota(jnp.int32, comb_ref.shape, 1)
    c = jnp.sum(jnp.where(lane == e, comb_ref[...], 0.0), axis=1, keepdims=True)
    acc_ref[...] += c * y

    @pl.when(e == n_experts - 1)
    def _():
        o_ref[...] = _layer_norm(alpha * h_ref[...] + acc_ref[...], g2_ref[...], b2_ref[...])


def _moe(h, comb, weights, *, tm, alpha):
    wg, wu, bg, bu, wd, bd, g2, b2 = weights
    m, d_model = h.shape
    n_experts, _, d_ff = wg.shape
    return pl.pallas_call(
        functools.partial(_moe_kernel, alpha=alpha, n_experts=n_experts),
        grid=(m // tm, n_experts),
        in_specs=[pl.BlockSpec((tm, d_model), lambda i, e: (i, 0)),
                  pl.BlockSpec((tm, LANES), lambda i, e: (i, 0)),
                  pl.BlockSpec((1, d_model, d_ff), lambda i, e: (e, 0, 0)),
                  pl.BlockSpec((1, d_model, d_ff), lambda i, e: (e, 0, 0)),
                  pl.BlockSpec((1, 1, d_ff), lambda i, e: (e, 0, 0)),
                  pl.BlockSpec((1, 1, d_ff), lambda i, e: (e, 0, 0)),
                  pl.BlockSpec((1, d_ff, d_model), lambda i, e: (e, 0, 0)),
                  pl.BlockSpec((1, 1, d_model), lambda i, e: (e, 0, 0)),
                  pl.BlockSpec((1, d_model), lambda i, e: (0, 0)),
                  pl.BlockSpec((1, d_model), lambda i, e: (0, 0))],
        out_specs=pl.BlockSpec((tm, d_model), lambda i, e: (i, 0)),
        out_shape=jax.ShapeDtypeStruct((m, d_model), F32),
        scratch_shapes=[pltpu.VMEM((tm, d_model), BF16), pltpu.VMEM((tm, d_model), F32)],
        compiler_params=_cparams(("arbitrary", "arbitrary")),
        name="moe_ln",
    )(h, comb, wg, wu, bg, bu, wd, bd, g2, b2)


def _gather_cmp_kernel(pt_ref, *refs, n_steps):
    page_refs = refs[:PAGES_PER_STEP]
    new_ref, o_ref, rows_ref = refs[PAGES_PER_STEP:]
    j = pl.program_id(1)

    n_lane_tiles = NSA_KV_WIDTH // LANES

    @pl.when(j < n_steps)
    def _():
        for i in range(PAGES_PER_STEP):
            x = page_refs[i][0].T
            for c in range(n_lane_tiles):
                rows_ref[c, i * PAGE_SIZE:(i + 1) * PAGE_SIZE, :] = x[:, c * LANES:(c + 1) * LANES]

    @pl.when(j == n_steps)
    def _():
        rows_ref[...] = jnp.zeros(rows_ref.shape, F32)
        for c in range(n_lane_tiles):
            rows_ref[c, 0:8, :] = new_ref[0, :, c * LANES:(c + 1) * LANES]

    n_chunks = PAGES_PER_STEP * PAGE_SIZE // CMP_STRIDE
    for p in range(CMP_STRIDE):
        for c in range(n_lane_tiles):
            o_ref[0, :, p * NSA_KV_WIDTH + c * LANES:p * NSA_KV_WIDTH + (c + 1) * LANES] = rows_ref[
                c, pl.ds(p, n_chunks, stride=CMP_STRIDE), :].astype(BF16)


def _gather_sel_kernel(pt_ref, *refs, n_steps):
    page_refs = refs[:PAGES_PER_STEP]
    newk_ref, newv_ref, ka_ref, vot_ref = refs[PAGES_PER_STEP:]
    j = pl.program_id(1)
    tag_row = lax.broadcasted_iota(jnp.int32, (HEAD_DIM, PAGE_SIZE), 0)
    tag_col = jnp.right_shift(lax.broadcasted_iota(jnp.int32, (HEAD_DIM, PAGE_SIZE), 1), 6)
    pages_per_tile = SEL_TILE // PAGE_SIZE

    @pl.when(j < n_steps)
    def _():
        for i in range(PAGES_PER_STEP):
            xt = page_refs[i][0]
            tag = jnp.where(tag_row == (PAGE_SIZE // SEL_BLOCK) * (i % pages_per_tile) + tag_col, 1.0, 0.0)
            rs = slice(i * PAGE_SIZE, (i + 1) * PAGE_SIZE)
            for g in range(NSA_GROUPS):
                kt = xt[g * HEAD_DIM:(g + 1) * HEAD_DIM, :]
                vt = xt[(NSA_GROUPS + g) * HEAD_DIM:(NSA_GROUPS + g + 1) * HEAD_DIM, :]
                ka_ref[0, rs, g * LANES:(g + 1) * LANES] = jnp.concatenate([kt, tag], axis=0).T.astype(BF16)
                vot_ref[0, g * LANES:g * LANES + HEAD_DIM, rs] = vt.astype(BF16)
                vot_ref[0, g * LANES + HEAD_DIM:(g + 1) * LANES, rs] = jnp.ones((HEAD_DIM, PAGE_SIZE), BF16)

    @pl.when(j == n_steps)
    def _():
        ka_ref[...] = jnp.zeros(ka_ref.shape, BF16)
        vot_ref[...] = jnp.zeros(vot_ref.shape, BF16)
        ka_ref[0, 0:16, :] = newk_ref[0]
        vnew = newv_ref[0].astype(F32)
        vnew = jnp.concatenate([vnew, jnp.zeros((LANES - vnew.shape[0], NSA_PACK_WIDTH), F32)], axis=0)
        vot_ref[0, :, 0:LANES] = vnew.T.astype(BF16)


def _gather_nsa(cache_t, page_table, new_blocks, kernel_fn, outs, scratch, name):
    b, n_pages = page_table.shape
    assert n_pages % PAGES_PER_STEP == 0
    n_steps = n_pages // PAGES_PER_STEP

    def page_spec(i):
        return pl.BlockSpec((1, NSA_KV_WIDTH, PAGE_SIZE),
                            lambda s, j, pt: (pt[s, jnp.minimum(j * PAGES_PER_STEP + i, n_pages - 1)], 0, 0))

    def out_spec(shape, axis):
        return pl.BlockSpec((1,) + shape, lambda s, j, pt: (s, j, 0) if axis == 0 else (s, 0, j))

    def out_shape(shape, axis):
        full = list(shape)
        full[axis] *= n_steps + 1
        return jax.ShapeDtypeStruct((b,) + tuple(full), BF16)

    in_specs = [page_spec(i) for i in range(PAGES_PER_STEP)]
    in_specs += [pl.BlockSpec((1,) + nb.shape[1:], lambda s, j, pt: (s, 0, 0)) for nb in new_blocks]
    return pl.pallas_call(
        functools.partial(kernel_fn, n_steps=n_steps),
        grid_spec=pltpu.PrefetchScalarGridSpec(
            num_scalar_prefetch=1, grid=(b, n_steps + 1), in_specs=in_specs,
            out_specs=[out_spec(s, a) for s, a in outs], scratch_shapes=scratch),
        out_shape=[out_shape(s, a) for s, a in outs],
        compiler_params=_cparams(("arbitrary", "arbitrary")),
        name=name,
    )(page_table, *([cache_t] * PAGES_PER_STEP), *new_blocks)


def _nsa_cache_pages(cache_l):
    n_pool = cache_l.shape[0]
    return cache_l.transpose(0, 2, 3, 4, 1).reshape(n_pool, NSA_KV_WIDTH, PAGE_SIZE)


def _nsa_gates(y, off):
    b, t, _ = y.shape
    g = y[:, :, off + _OFF_GA:off + _OFF_GA + 3 * NSA_HEADS].reshape(b, t, 3, NSA_GROUPS, NSA_REP)
    g = g.transpose(0, 3, 1, 2, 4).reshape(b, NSA_GROUPS, t, 3 * NSA_REP)
    return jnp.pad(g, ((0, 0), (0, 0), (0, 0), (0, LANES - 3 * NSA_REP)))


def kernel(x_prompt, x_sample, cache_cmp_kv, cache_sel_kv, cache_sb_kv, state_win_kv, page_table, rel_bias,
           w_in, cmp_pe, cmp_w1, cmp_w2, w_up_nsa, w_up_sb, w_out, ln1_g, ln1_b,
           w_router, b_router, w_gate_up, b_gate_up, w_down, b_down, ln2_g, ln2_b):
    depth, d_model = w_in.shape[0], w_in.shape[1]
    bp, seq, _ = x_prompt.shape
    bs, dec_seq, _ = x_sample.shape
    assert bp == 1 and dec_seq == 1
    n_pages = page_table.shape[1]
    past = n_pages * PAGE_SIZE
    n_buf = state_win_kv.shape[2]
    n_experts = w_router.shape[2]
    n_pool = cache_cmp_kv.shape[1]
    alpha = (2 * depth) ** 0.25
    off = 2 * d_model
    tq_s = 16
    assert seq % SEL_TILE == 0 and past % SEL_TILE == 0

    hp = x_prompt.reshape(seq, d_model)
    hs = x_sample.reshape(bs, d_model)
    outs = [[] for _ in range(8)]
    for l in range(depth):
        w_proj = _proj_weight(w_in[l], d_model)
        cmp_w = _compress_weights(cmp_pe[l], cmp_w1[l], cmp_w2[l])
        fin_w = (w_up_nsa[l].astype(BF16), w_up_sb[l].astype(BF16), w_out[l].astype(BF16),
                 ln1_g[l][None], ln1_b[l][None],
                 jnp.pad(w_router[l], ((0, 0), (0, LANES - n_experts))).astype(BF16),
                 jnp.pad(b_router[l], (0, LANES - n_experts))[None])
        w_gate, w_upp = _split_gate_up(w_gate_up[l])
        moe_w = (w_gate, w_upp, b_gate_up[l][:, None, 0::2], b_gate_up[l][:, None, 1::2],
                 w_down[l].astype(BF16), b_down[l][:, None, :], ln2_g[l][None], ln2_b[l][None])

        tm = min(512, seq)
        y, yb = _in_proj(hp, w_proj, min(PROJ_TM, seq), PROJ_TN)
        kv_c = y[:, off + _OFF_KVC:off + _OFF_KVC + NSA_KV_WIDTH]
        kv_s = y[:, off + _OFF_KVS:off + _OFF_KVS + NSA_KV_WIDTH]
        kv_w = y[:, off + _OFF_KVW:off + _OFF_KVW + NSA_KV_WIDTH]
        kv_b = y[:, off + _OFF_KVB:off + _OFF_KVB + 2 * SB_WIDTH]
        q_pk, ka, vot, wk, wvt = _pack(yb, off, tm, 0, 1, True)
        n_chunks = seq // CMP_STRIDE
        chunks = yb[:, off + _OFF_KVC:off + _OFF_KVC + NSA_KV_WIDTH].reshape(1, n_chunks, CHUNK_WIDTH)
        cmpkv = _compress(chunks, seq, cmp_w)
        o_a = _nsa_attention(rel_bias, q_pk[None], _nsa_gates(y[None], off), cmpkv, ka[None], vot[None],
                             wk[None], wvt[None], n_real=KEY_TILE, q_pos0=0, win_pos0=0, n_sel_rows=seq,
                             skip_empty=False)
        o_b = _sb_attention(yb[None], off, tq=SB_PROMPT_TQ)
        h1, comb = _finish(o_a[0], o_b[0], y, hp, fin_w, tm=min(256, seq), d_model=d_model, alpha=alpha,
                           n_experts=n_experts)
        hp = _moe(h1, comb, moe_w, tm=min(512, seq), alpha=alpha)
        n_win = min(WINDOW, seq)
        outs[0].append(kv_c.reshape(1, seq, 2, NSA_GROUPS, HEAD_DIM))
        outs[1].append(kv_s.reshape(1, seq, 2, NSA_GROUPS, HEAD_DIM))
        outs[2].append(kv_b.reshape(1, seq, 2, SB_HEADS, SB_HEAD_DIM))
        outs[3].append(kv_w[seq - n_win:].reshape(1, n_win, 2, NSA_GROUPS, HEAD_DIM))

        ys, ysb = _in_proj(hs, w_proj, bs, PROJ_TN)
        kv_c = ys[:, off + _OFF_KVC:off + _OFF_KVC + NSA_KV_WIDTH]
        kv_s = ys[:, off + _OFF_KVS:off + _OFF_KVS + NSA_KV_WIDTH]
        kv_w = ys[:, off + _OFF_KVW:off + _OFF_KVW + NSA_KV_WIDTH]
        kv_b = ys[:, off + _OFF_KVB:off + _OFF_KVB + 2 * SB_WIDTH]
        q_pk, ka_new, vo_new, _, _ = _pack(ysb, off, bs, past, 0, False)
        rows_step = PAGES_PER_STEP * PAGE_SIZE
        (chunks,) = _gather_nsa(_nsa_cache_pages(cache_cmp_kv[l]), page_table,
                                [_pad_rows(kv_c[:, None, :], 1, 8)], _gather_cmp_kernel,
                                [((rows_step // CMP_STRIDE, CHUNK_WIDTH), 0)],
                                [pltpu.VMEM((NSA_KV_WIDTH // LANES, rows_step, LANES), F32)], "gather_cmp")
        ka, vot = _gather_nsa(_nsa_cache_pages(cache_sel_kv[l]), page_table,
                              [_pad_rows(ka_new[:, None, :], 1, 16), _pad_rows(vo_new[:, None, :], 1, 16)],
                              _gather_sel_kernel, [((rows_step, NSA_PACK_WIDTH), 0), ((NSA_PACK_WIDTH, rows_step), 1)],
                              [], "gather_sel")
        cmpkv = _compress(chunks, past + 1, cmp_w)
        win_all = jnp.concatenate([state_win_kv[l].reshape(bs, n_buf, NSA_KV_WIDTH), kv_w[:, None, :]], axis=1)
        lw = _round_up(n_buf + 1, KEY_TILE)
        win5 = win_all.reshape(bs, n_buf + 1, 2, NSA_GROUPS, HEAD_DIM).astype(BF16)
        wk = jnp.pad(win5[:, :, 0], ((0, 0), (0, lw - n_buf - 1), (0, 0), (0, LANES - HEAD_DIM)))
        wk = wk.reshape(bs, lw, NSA_PACK_WIDTH)
        wvt = jnp.concatenate([win5[:, :, 1], jnp.ones_like(win5[:, :, 1])], axis=-1)
        wvt = _pad_rows(wvt.transpose(0, 2, 3, 1).reshape(bs, NSA_PACK_WIDTH, n_buf + 1), 2, lw)
        o_a = _nsa_attention(rel_bias, _pad_rows(q_pk[:, None, :], 1, KEY_TILE),
                             _pad_rows(_nsa_gates(ys[:, None, :], off), 2, KEY_TILE),
                             cmpkv, ka, vot, wk, wvt, n_real=1, q_pos0=past, win_pos0=past - n_buf,
                             n_sel_rows=past + 1, skip_empty=True)
        q_b = ysb[:, off + _OFF_QB:off + _OFF_QB + SB_WIDTH].reshape(bs, SB_HEADS, 1, SB_HEAD_DIM)
        o_b = _sb_decode(cache_sb_kv[l].reshape(n_pool, PAGE_SIZE * 2 * SB_HEADS, SB_HEAD_DIM), page_table,
                         _pad_rows(q_b, 2, tq_s))
        h1, comb = _finish(o_a[:, 0], o_b.reshape(bs, SB_WIDTH).astype(BF16), ys, hs, fin_w, tm=bs,
                           d_model=d_model, alpha=alpha, n_experts=n_experts)
        hs = _moe(h1, comb, moe_w, tm=bs, alpha=alpha)
        outs[4].append(kv_c.reshape(bs, 1, 2, NSA_GROUPS, HEAD_DIM))
        outs[5].append(kv_s.reshape(bs, 1, 2, NSA_GROUPS, HEAD_DIM))
        outs[6].append(kv_b.reshape(bs, 1, 2, SB_HEADS, SB_HEAD_DIM))
        outs[7].append(win_all[:, 1:].reshape(bs, n_buf, 2, NSA_GROUPS, HEAD_DIM))

    return (hp.reshape(bp, seq, d_model), hs.reshape(bs, dec_seq, d_model)) + tuple(jnp.stack(o) for o in outs)
```

```python
import functools
import math

import numpy as np
import jax
import jax.numpy as jnp
from jax import lax
from jax.experimental import pallas as pl
from jax.experimental.pallas import tpu as pltpu

F32 = jnp.float32
BF16 = jnp.bfloat16

NSA_HEADS = 16
NSA_GROUPS = 4
NSA_REP = NSA_HEADS // NSA_GROUPS
HEAD_DIM = 64
CMP_BLOCK = 32
CMP_STRIDE = 16
CMP_HIDDEN = 64
SEL_BLOCK = 64
SEL_TOPN = 16
WINDOW = 512
SB_HEADS = 8
SB_HEAD_DIM = 128
REL_BUCKETS = 32
REL_MAX_DIST = 4096
TOP_K = 4
SWIGLU_LIMIT = 7.0
SWIGLU_ALPHA = 1.702
LN_EPS = 1e-5
NEG_INF = -1e30
FORCED_SCORE = 1e4
PAGE_SIZE = 128

LANES = 128
KEY_TILE = 128
SEL_TILE = 512
SEL_TILE_BLOCKS = SEL_TILE // SEL_BLOCK
NSA_WIDTH = NSA_HEADS * HEAD_DIM
SB_WIDTH = SB_HEADS * SB_HEAD_DIM
NSA_KV_WIDTH = 2 * NSA_GROUPS * HEAD_DIM
NSA_PACK_WIDTH = NSA_GROUPS * LANES
CHUNK_WIDTH = CMP_STRIDE * NSA_KV_WIDTH
CMP_PRE_WIDTH = 2 * NSA_GROUPS * CMP_HIDDEN
VMEM_LIMIT = 56 * 1024 * 1024
PAGES_PER_STEP = 16
SB_PROMPT_TQ = 512
EXP_UNDERFLOW = -104.0

_MAX_EXACT = REL_BUCKETS // 2
_T5_THRESH = tuple(int(math.ceil(_MAX_EXACT * 2.0 ** (j / 2.0) - 1e-9)) for j in range(1, REL_BUCKETS - _MAX_EXACT))
_FAR_DIST = _T5_THRESH[-1]
N_BIAS_TILES = -(-(_FAR_DIST + KEY_TILE) // KEY_TILE) + 1
WIN_TILES = WINDOW // KEY_TILE + 1
CMP_TAB_M_MAX = (_FAR_DIST + CMP_BLOCK - 1 + CMP_STRIDE * (LANES - 1) - 1) // KEY_TILE
CMP_TAB_X0 = -(-(KEY_TILE // CMP_STRIDE) * CMP_TAB_M_MAX // LANES) * LANES
CMP_TAB_WIDTH = CMP_TAB_X0 + 2 * LANES

_OFF_QA = 0
_OFF_KVC = _OFF_QA + NSA_WIDTH
_OFF_KVS = _OFF_KVC + NSA_KV_WIDTH
_OFF_KVW = _OFF_KVS + NSA_KV_WIDTH
_OFF_GA = _OFF_KVW + NSA_KV_WIDTH
_OFF_QB = _OFF_GA + LANES
_OFF_KVB = _OFF_QB + SB_WIDTH
_OFF_END = _OFF_KVB + 2 * SB_WIDTH
PROJ_TN = 768
PROJ_TM = 1024


def _cparams(sem):
    return pltpu.CompilerParams(dimension_semantics=sem, vmem_limit_bytes=VMEM_LIMIT)


def _round_up(x, m):
    return -(-x // m) * m


def _dot(a, b):
    return jnp.dot(a, b, preferred_element_type=F32)


def _dot_nt(a, b):
    return lax.dot_general(a, b, (((1,), (1,)), ((), ())), preferred_element_type=F32)


def _layer_norm(x, g, b):
    mu = jnp.mean(x, axis=-1, keepdims=True)
    xc = x - mu
    var = jnp.mean(xc * xc, axis=-1, keepdims=True)
    return xc * lax.rsqrt(var + LN_EPS) * g + b


def _pad_rows(x, axis, n):
    pad = [(0, 0)] * x.ndim
    pad[axis] = (0, n - x.shape[axis])
    return jnp.pad(x, pad)


def _matmul_kernel(x_ref, w_ref, o_ref, ob_ref, xb_ref):
    @pl.when(pl.program_id(1) == 0)
    def _():
        xb_ref[...] = x_ref[...].astype(BF16)

    acc = _dot(xb_ref[...], w_ref[...])
    o_ref[...] = acc
    ob_ref[...] = acc.astype(BF16)


def _in_proj(x, w, tm, tn):
    m, k = x.shape
    n = w.shape[1]
    return pl.pallas_call(
        _matmul_kernel,
        grid=(m // tm, n // tn),
        in_specs=[pl.BlockSpec((tm, k), lambda i, j: (i, 0)),
                  pl.BlockSpec((k, tn), lambda i, j: (0, j))],
        out_specs=[pl.BlockSpec((tm, tn), lambda i, j: (i, j)),
                   pl.BlockSpec((tm, tn), lambda i, j: (i, j))],
        out_shape=[jax.ShapeDtypeStruct((m, n), F32), jax.ShapeDtypeStruct((m, n), BF16)],
        scratch_shapes=[pltpu.VMEM((tm, k), BF16)],
        compiler_params=_cparams(("arbitrary", "arbitrary")),
        name="in_proj",
    )(x, w)


def _proj_weight(w_in_l, d_model):
    o = np.cumsum((0, NSA_WIDTH, NSA_KV_WIDTH, NSA_KV_WIDTH, NSA_KV_WIDTH, 3 * NSA_HEADS,
                   SB_WIDTH, 2 * SB_WIDTH, 2 * d_model))
    g_m = w_in_l[:, o[7]:o[8]]
    head = w_in_l[:, o[0]:o[5]]
    tail = w_in_l[:, o[5]:o[7]]
    pad_a = jnp.zeros((d_model, LANES - 3 * NSA_HEADS), w_in_l.dtype)
    n = 2 * d_model + _OFF_END
    n_pad = _round_up(n, PROJ_TN)
    pad_b = jnp.zeros((d_model, n_pad - n), w_in_l.dtype)
    return jnp.concatenate([g_m, head, pad_a, tail, pad_b], axis=1).astype(BF16)


def _pack_kernel(qa_ref, qb_ref, ks_ref, kw_ref, pq_ref, pk_ref, pv_ref,
                 q_ref, ka_ref, vo_ref, wk_ref, wvo_ref, *, tm, pos0, pos_stride, transpose_vo):
    q_ref[:, 0:NSA_HEADS * LANES // 2] = _dot(qa_ref[...], pq_ref[...]).astype(BF16)
    q_ref[:, NSA_HEADS * LANES // 2:] = _dot(qb_ref[...], pq_ref[...]).astype(BF16)
    lane = lax.broadcasted_iota(jnp.int32, (tm, NSA_PACK_WIDTH), 1) % LANES
    pos = pos0 + pos_stride * (pl.program_id(0) * tm + lax.broadcasted_iota(jnp.int32, (tm, NSA_PACK_WIDTH), 0))
    local_blk = jnp.right_shift(pos % SEL_TILE, 6)
    tag = (lane >= HEAD_DIM) & (lane - HEAD_DIM == local_blk)
    ka_ref[...] = jnp.where(tag, 1.0, _dot(ks_ref[...], pk_ref[...])).astype(BF16)
    vo = jnp.where(lane >= HEAD_DIM, 1.0, _dot(ks_ref[...], pv_ref[...]))
    vo_ref[...] = (vo.T if transpose_vo else vo).astype(BF16)
    wk_ref[...] = _dot(kw_ref[...], pk_ref[...]).astype(BF16)
    wvo = jnp.where(lane >= HEAD_DIM, 1.0, _dot(kw_ref[...], pv_ref[...]))
    wvo_ref[...] = (wvo.T if transpose_vo else wvo).astype(BF16)


def _pack_matrices():
    half = NSA_WIDTH // 2
    pq = np.zeros((half, half * 2), np.float32)
    for h in range(NSA_HEADS // 2):
        for d in range(HEAD_DIM):
            pq[h * HEAD_DIM + d, h * LANES + d] = HEAD_DIM ** -0.5
    pk = np.zeros((NSA_KV_WIDTH, NSA_PACK_WIDTH), np.float32)
    pv = np.zeros((NSA_KV_WIDTH, NSA_PACK_WIDTH), np.float32)
    for g in range(NSA_GROUPS):
        for d in range(HEAD_DIM):
            pk[g * HEAD_DIM + d, g * LANES + d] = 1.0
            pv[NSA_GROUPS * HEAD_DIM + g * HEAD_DIM + d, g * LANES + d] = 1.0
    return tuple(jnp.asarray(a, dtype=BF16) for a in (pq, pk, pv))


def _pack(yb, off, tm, pos0, pos_stride, transpose_vo):
    m = yb.shape[0]
    assert off % NSA_KV_WIDTH == 0 and SEL_TILE_BLOCKS <= LANES - HEAD_DIM
    c0 = off // NSA_KV_WIDTH
    mats = _pack_matrices()
    blk = lambda c: pl.BlockSpec((tm, NSA_KV_WIDTH), lambda i, c=c: (i, c))
    full = lambda a: pl.BlockSpec(a.shape, lambda i: (0, 0))
    rows_spec = lambda w: pl.BlockSpec((tm, w), lambda i: (i, 0))
    rows_shape = lambda w: jax.ShapeDtypeStruct((m, w), BF16)
    vo_spec = pl.BlockSpec((NSA_PACK_WIDTH, tm), lambda i: (0, i)) if transpose_vo else rows_spec(NSA_PACK_WIDTH)
    vo_shape = jax.ShapeDtypeStruct((NSA_PACK_WIDTH, m), BF16) if transpose_vo else rows_shape(NSA_PACK_WIDTH)
    return pl.pallas_call(
        functools.partial(_pack_kernel, tm=tm, pos0=pos0, pos_stride=pos_stride, transpose_vo=transpose_vo),
        grid=(m // tm,),
        in_specs=[blk(c0), blk(c0 + 1), blk(c0 + _OFF_KVS // NSA_KV_WIDTH), blk(c0 + _OFF_KVW // NSA_KV_WIDTH)]
        + [full(a) for a in mats],
        out_specs=[rows_spec(NSA_HEADS * LANES), rows_spec(NSA_PACK_WIDTH), vo_spec, rows_spec(NSA_PACK_WIDTH), vo_spec],
        out_shape=[rows_shape(NSA_HEADS * LANES), rows_shape(NSA_PACK_WIDTH), vo_shape, rows_shape(NSA_PACK_WIDTH),
                   vo_shape],
        compiler_params=_cparams(("arbitrary",)),
        name="nsa_pack",
    )(yb, yb, yb, yb, *mats)


def _compress_kernel(x_ref, pe_ref, w1_ref, w2_ref, o_ref, acc_ref, pacc_ref, *, nk, nb):
    kk = pl.program_id(1)

    @pl.when(kk == 0)
    def _():
        acc_ref[...] = jnp.zeros_like(acc_ref)
        pacc_ref[...] = jnp.zeros_like(pacc_ref)

    kv_in = NSA_GROUPS * HEAD_DIM
    kv_out = 2 * NSA_GROUPS * CMP_HIDDEN
    for j in range(x_ref.shape[2] // kv_in):
        rows_j = slice(j * kv_in, (j + 1) * kv_in)
        cols_j = slice((j % 2) * kv_out, (j % 2 + 1) * kv_out)
        acc_ref[:, cols_j] += _dot(x_ref[0, :, rows_j], w1_ref[rows_j, :])
        pacc_ref[:, cols_j] += _dot(pe_ref[:, rows_j].astype(BF16), w1_ref[rows_j, :])

    @pl.when(kk == nk - 1)
    def _():
        h = NSA_GROUPS * CMP_HIDDEN
        pres = []
        for kv in range(2):
            c0 = kv * kv_out
            pe_bias = pacc_ref[0:1, c0:c0 + h] + pacc_ref[1:2, c0 + h:c0 + 2 * h]
            pres.append(acc_ref[0:nb, c0:c0 + h] + acc_ref[1:nb + 1, c0 + h:c0 + 2 * h] + pe_bias)
        act = jax.nn.gelu(jnp.concatenate(pres, axis=1))
        o_ref[0] = _dot(act.astype(BF16), w2_ref[...]).astype(BF16)


def _compress_weights(cmp_pe_l, cmp_w1_l, cmp_w2_l):
    halves = CMP_BLOCK // CMP_STRIDE
    w1h = cmp_w1_l.reshape(2, halves, CMP_STRIDE, HEAD_DIM, CMP_HIDDEN)
    eye_k = jnp.eye(2, dtype=F32)
    eye_g = jnp.eye(NSA_GROUPS, dtype=F32)
    w1big = jnp.einsum('khpdf,gG->pkGdhgf', w1h, eye_g).reshape(CHUNK_WIDTH, halves * NSA_GROUPS * CMP_HIDDEN)
    w2big = jnp.einsum('kfd,kK,gG->kgfGKd', cmp_w2_l, eye_k, eye_g).reshape(CMP_PRE_WIDTH, NSA_PACK_WIDTH)
    pe = cmp_pe_l.reshape(2, halves, CMP_STRIDE, HEAD_DIM).transpose(1, 2, 0, 3)
    pe = jnp.broadcast_to(pe[:, :, :, None, :], (halves, CMP_STRIDE, 2, NSA_GROUPS, HEAD_DIM))
    pe_rows = jnp.pad(pe.reshape(halves, CHUNK_WIDTH), ((0, 8 - halves), (0, 0)))
    return pe_rows, w1big.astype(BF16), w2big.astype(BF16)


def _compress(chunks, n_rows, weights):
    pe_rows, w1big, w2big = weights
    b = chunks.shape[0]
    n_chunks = -(-n_rows // CMP_STRIDE)
    nb = _round_up(n_chunks - 1, LANES)
    ncp = nb + 16
    if chunks.shape[1] < ncp:
        chunks = _pad_rows(chunks, 1, ncp)
    kstep = 2048
    nk = CHUNK_WIDTH // kstep
    nw = w1big.shape[1]
    return pl.pallas_call(
        functools.partial(_compress_kernel, nk=nk, nb=nb),
        grid=(b, nk),
        in_specs=[pl.BlockSpec((1, ncp, kstep), lambda i, k: (i, 0, k)),
                  pl.BlockSpec((8, kstep), lambda i, k: (0, k)),
                  pl.BlockSpec((kstep, nw), lambda i, k: (k, 0)),
                  pl.BlockSpec(w2big.shape, lambda i, k: (0, 0))],
        out_specs=pl.BlockSpec((1, nb, NSA_PACK_WIDTH), lambda i, k: (i, 0, 0)),
        out_shape=jax.ShapeDtypeStruct((b, nb, NSA_PACK_WIDTH), BF16),
        scratch_shapes=[pltpu.VMEM((ncp, 2 * nw), F32), pltpu.VMEM((8, 2 * nw), F32)],
        compiler_params=_cparams(("arbitrary", "arbitrary")),
        name="nsa_compress",
    )(chunks, pe_rows, w1big, w2big)


def _t5_bucket(d):
    n = jnp.maximum(d, 0)
    cnt = jnp.zeros_like(n)
    for th in _T5_THRESH:
        cnt = cnt + jnp.where(n >= th, 1, 0)
    return jnp.where(n < _MAX_EXACT, n, _MAX_EXACT + cnt)


def _bias_tiles(d, rel_ref, g):
    bucket = _t5_bucket(d)
    outs = [jnp.zeros(d.shape, F32) for _ in range(NSA_REP)]
    for j in range(REL_BUCKETS):
        hit = bucket == j
        for r in range(NSA_REP):
            outs[r] = jnp.where(hit, rel_ref[j, NSA_REP * g + r], outs[r])
    return tuple(outs)


def _sel_head_step(st_ref, adjust, shift, cs, mt_ref, acct_ref, pt_ref):
    chunks = [slice(c * KEY_TILE, (c + 1) * KEY_TILE) for c in range(SEL_TILE // KEY_TILE)]
    mx = None
    for c, ks in enumerate(chunks):
        x = st_ref[ks, cs]
        if adjust is not None:
            x = adjust(c, x)
            st_ref[ks, cs] = x
        mx = x if mx is None else jnp.maximum(mx, x)
    m_old = mt_ref[:, cs]
    m_new = jnp.maximum(m_old, jnp.max(mx, axis=0, keepdims=True) + shift)
    acct_ref[:, cs] = jnp.exp(m_old - m_new) * acct_ref[:, cs]
    off = m_new - shift
    for ks in chunks:
        pt_ref[ks, cs] = jnp.exp(st_ref[ks, cs] - off).astype(BF16)
    mt_ref[:, cs] = m_new


def _nsa_kernel(rel_ref, q_ref, gate_ref, cmp_ref, ka_ref, vot_ref, *rest,
                tq, n_real, q_pos0, win_pos0, nb, nsbp, n_sb, ls, lw, skip_empty):
    wk_refs = rest[:WIN_TILES]
    wvt_refs = rest[WIN_TILES:2 * WIN_TILES]
    (ov_ref, o_ref, tabt_ref, ctab_ref, wbias_ref, s_ref, pent_ref, qat_ref, st_ref, pt_ref, st2_ref, pt2_ref,
     acct_ref, mt_ref) = rest[2 * WIN_TILES:]
    g = pl.program_id(0)
    b = pl.program_id(1)
    qt = pl.program_id(2)
    t0 = q_pos0 + qt * tq
    rows = lax.broadcasted_iota(jnp.int32, (tq, LANES), 0)
    cols = lax.broadcasted_iota(jnp.int32, (tq, LANES), 1)
    tpos = t0 + rows
    far_idx = N_BIAS_TILES - 1

    @pl.when((b == 0) & (qt == 0))
    def _():
        def body(di, c):
            outs_t = _bias_tiles(di * KEY_TILE + cols - rows, rel_ref, g)
            for r in range(NSA_REP):
                tabt_ref[di, r] = outs_t[r]
            return c
        lax.fori_loop(0, N_BIAS_TILES, body, 0)
        for i in range(WIN_TILES):
            d = (WIN_TILES - 1 - i) * KEY_TILE + cols - rows
            outs = _bias_tiles(d, rel_ref, g)
            ok = (d >= 0) & (d < WINDOW)
            for r in range(NSA_REP):
                wbias_ref[r, i * KEY_TILE:(i + 1) * KEY_TILE, :] = jnp.where(ok, outs[r], NEG_INF)
        for xt in range(CMP_TAB_WIDTH // LANES):
            d = rows - (CMP_BLOCK - 1) + CMP_STRIDE * (CMP_TAB_X0 - (xt * LANES + cols))
            outs = _bias_tiles(d, rel_ref, g)
            for r in range(NSA_REP):
                ctab_ref[r, :, xt * LANES:(xt + 1) * LANES] = outs[r]

    qs = [q_ref[0, :, r * LANES:(r + 1) * LANES] for r in range(NSA_REP)]
    q = jnp.concatenate(qs, axis=0)
    row_slices = [pl.ds(r * tq, tq) for r in range(NSA_REP)]

    for ct in range(nb // LANES):
        cs_ct = slice(ct * LANES, (ct + 1) * LANES)
        live = t0 + tq - 1 >= CMP_STRIDE * LANES * ct + CMP_BLOCK - 1

        @pl.when(live)
        def _(ct=ct, cs_ct=cs_ct):
            s = _dot_nt(q, cmp_ref[0, cs_ct, :])
            d = tpos - (CMP_BLOCK - 1) - CMP_STRIDE * (ct * LANES + cols)
            m_idx = jnp.right_shift(t0, 7) - (LANES * CMP_STRIDE // KEY_TILE) * ct

            def far_fn():
                return tuple(jnp.full((tq, LANES), rel_ref[REL_BUCKETS - 1, NSA_REP * g + r], F32)
                             for r in range(NSA_REP))

            def near_fn():
                start = CMP_TAB_X0 - (KEY_TILE // CMP_STRIDE) * m_idx
                a0 = pl.multiple_of(jnp.right_shift(start, 7) * LANES, LANES)
                shift = (2 * LANES - (start - a0)) % (2 * LANES)
                return tuple(pltpu.roll(ctab_ref[r, :, pl.ds(a0, 2 * LANES)], shift, 1)[:, :LANES]
                             for r in range(NSA_REP))

            bias = lax.cond((m_idx >= 0) & (m_idx <= CMP_TAB_M_MAX), near_fn, far_fn)
            valid = d >= 0
            for r in range(NSA_REP):
                s_ref[row_slices[r], cs_ct] = jnp.where(valid, s[r * tq:(r + 1) * tq] + bias[r], NEG_INF)

        @pl.when(jnp.logical_not(live))
        def _(cs_ct=cs_ct):
            s_ref[:, cs_ct] = jnp.full((NSA_REP * tq, LANES), NEG_INF, F32)

    s = s_ref[:, 0:nb]
    valid = s > 0.5 * NEG_INF
    p = jnp.where(valid, jnp.exp(s - jnp.max(s, axis=1, keepdims=True)), 0.0)
    pc = p / jnp.maximum(jnp.sum(p, axis=1, keepdims=True), 1e-30)
    o_cmp = _dot(pc.astype(BF16), cmp_ref[0])

    psum = pc[0:tq] + pc[tq:2 * tq] + pc[2 * tq:3 * tq] + pc[3 * tq:4 * tq]
    p_hi = psum.astype(BF16)
    p_lo = (psum - p_hi.astype(F32)).astype(BF16)
    imp = _dot(p_hi, ov_ref[...]) + _dot(p_lo, ov_ref[...])
    blk = lax.broadcasted_iota(jnp.int32, (tq, nsbp), 1)
    blk_f = blk.astype(F32)
    sel_rows = lax.broadcasted_iota(jnp.int32, (tq, nsbp), 0)
    qb = jnp.right_shift(t0 + sel_rows, 6)
    forced = (blk == 0) | (blk == qb) | (blk == qb - 1)
    score = jnp.where(forced, FORCED_SCORE, jnp.where(blk <= qb, imp, -1.0))
    score = jnp.where(blk < n_sb, score, -2.0)
    score = score.T
    blk_t = lax.broadcasted_iota(jnp.int32, (nsbp, tq), 0).astype(F32)
    sel_t = jnp.zeros((nsbp, tq), F32)
    for _ in range(min(SEL_TOPN, n_sb)):
        mx = jnp.max(score, axis=0, keepdims=True)
        first = jnp.min(jnp.where(score == mx, blk_t, 1e9), axis=0, keepdims=True)
        pick = blk_t == first
        sel_t = jnp.where(pick, 1.0, sel_t)
        score = jnp.where(pick, -3.0, score)
    if n_real < tq:
        real_q = lax.broadcasted_iota(jnp.int32, (nsbp, tq), 1) < n_real
        sel_t = jnp.where(real_q, sel_t, 0.0)
    pent_ref[...] = (sel_t - 1.0) * 1e30

    mt_ref[...] = jnp.full(mt_ref.shape, NEG_INF, F32)
    acct_ref[...] = jnp.zeros(acct_ref.shape, F32)
    qat_ref[...] = q.astype(F32).T
    col_slices = [slice(r * tq, (r + 1) * tq) for r in range(NSA_REP)]

    kb_first = (q_pos0 - win_pos0) // KEY_TILE - (WIN_TILES - 1) + qt
    st_ref[...] = _dot(jnp.concatenate([wk_refs[i][0] for i in range(WIN_TILES)], axis=0),
                       qat_ref[...].astype(BF16))
    for r in range(NSA_REP):
        parts = []
        for i in range(WIN_TILES):
            ks = slice(i * KEY_TILE, (i + 1) * KEY_TILE)
            in_range = (kb_first + i >= 0) & (kb_first + i < lw // KEY_TILE)
            parts.append(jnp.where(in_range, st_ref[ks, col_slices[r]] + wbias_ref[r, ks, :], NEG_INF))
        st = jnp.concatenate(parts, axis=0)
        pt_ref[:, col_slices[r]] = jnp.exp(st - jnp.max(st, axis=0, keepdims=True)).astype(BF16)
    acc_w = _dot(jnp.concatenate([wvt_refs[i][0] for i in range(WIN_TILES)], axis=1), pt_ref[...])
    o_win_t = acc_w[0:HEAD_DIM] / jnp.maximum(acc_w[HEAD_DIM:], 1e-30)

    key_i = lax.broadcasted_iota(jnp.int32, (KEY_TILE, tq), 0)
    qry_i = lax.broadcasted_iota(jnp.int32, (KEY_TILE, tq), 1)
    real_pen = lax.broadcasted_iota(jnp.int32, (SEL_TILE_BLOCKS, tq), 1) < n_real

    def tile_pen(kt):
        return pent_ref[pl.ds(pl.multiple_of(kt * SEL_TILE_BLOCKS, SEL_TILE_BLOCKS), SEL_TILE_BLOCKS), :]

    def tile_scores(kt, pen, slot):
        for r in range(NSA_REP):
            qat_ref[HEAD_DIM:HEAD_DIM + SEL_TILE_BLOCKS, col_slices[r]] = pen
        st2_ref[slot] = _dot(ka_ref[0, pl.ds(pl.multiple_of(kt * SEL_TILE, SEL_TILE), SEL_TILE), :],
                             qat_ref[...].astype(BF16))

    def tile_softmax(kt, slot, near):
        k0 = kt * SEL_TILE
        di0 = jnp.right_shift(t0 - k0, 7)
        for r in range(NSA_REP):
            if near:
                def adjust(c4, x, r=r):
                    bias = tabt_ref[jnp.clip(di0 - c4, 0, far_idx), r]
                    return jnp.where(k0 + c4 * KEY_TILE + key_i <= t0 + qry_i, x + bias, NEG_INF)
                shift = 0.0
            else:
                adjust = None
                shift = rel_ref[REL_BUCKETS - 1, NSA_REP * g + r]
            _sel_head_step(st2_ref.at[slot], adjust, shift, col_slices[r], mt_ref, acct_ref, pt2_ref.at[slot])

    def tile_values(kt, slot):
        acct_ref[...] += _dot(vot_ref[0, :, pl.ds(pl.multiple_of(kt * SEL_TILE, SEL_TILE), SEL_TILE)], pt2_ref[slot])

    n_kt = jnp.minimum(jnp.right_shift(t0 + tq - 1, 9) + 1, ls // SEL_TILE)
    n_far = jnp.clip(jnp.right_shift(t0 - _FAR_DIST - (SEL_TILE - 1), 9) + 1, 0, n_kt)

    if skip_empty:
        def sel_body(kt, c):
            pen = tile_pen(kt)

            @pl.when(jnp.max(jnp.where(real_pen, pen, NEG_INF)) > -1.0)
            def _():
                tile_scores(kt, pen, 0)
                pl.when(kt < n_far)(lambda: tile_softmax(kt, 0, False))
                pl.when(kt >= n_far)(lambda: tile_softmax(kt, 0, True))
                tile_values(kt, 0)
            return c
        lax.fori_loop(0, n_kt, sel_body, 0)
    else:
        def run(lo, hi, near):
            @pl.when(hi > lo)
            def _():
                pt2_ref[1] = jnp.zeros((SEL_TILE, NSA_REP * tq), BF16)
                tile_scores(lo, tile_pen(lo), 0)

                def body(p, c):
                    i = lo + 2 * p
                    nxt = jnp.minimum(i + 1, hi - 1)
                    tile_scores(nxt, tile_pen(nxt), 1)
                    tile_values(jnp.maximum(i - 1, lo), 1)
                    tile_softmax(i, 0, near)

                    @pl.when(i + 1 < hi)
                    def _():
                        nx2 = jnp.minimum(i + 2, hi - 1)
                        tile_scores(nx2, tile_pen(nx2), 0)
                        tile_values(i, 0)
                        tile_softmax(i + 1, 1, near)
                    return c
                lax.fori_loop(0, jnp.right_shift(hi - lo + 1, 1), body, 0)
                last_odd = (hi - 1 - lo) % 2 == 1
                pl.when(last_odd)(lambda: tile_values(hi - 1, 1))
                pl.when(jnp.logical_not(last_odd))(lambda: tile_values(hi - 1, 0))

        run(0, n_far, False)
        run(n_far, n_kt, True)
    acct = acct_ref[...]
    o_sel_t = acct[0:HEAD_DIM] / jnp.maximum(acct[HEAD_DIM:], 1e-30)
    o_sel = jnp.concatenate([jnp.zeros_like(o_sel_t), o_sel_t], axis=0).T
    o_win = jnp.concatenate([jnp.zeros_like(o_win_t), o_win_t], axis=0).T

    gt = jax.nn.sigmoid(gate_ref[0, 0])
    heads = []
    for r in range(NSA_REP):
        rs = slice(r * tq, (r + 1) * tq)
        heads.append(gt[:, r:r + 1] * o_cmp[rs] + gt[:, NSA_REP + r:NSA_REP + r + 1] * o_sel[rs]
                     + gt[:, 2 * NSA_REP + r:2 * NSA_REP + r + 1] * o_win[rs])
    low = cols < HEAD_DIM
    for pair in range(NSA_REP // 2):
        left = pltpu.roll(heads[2 * pair], HEAD_DIM, 1)
        o_ref[0, :, pair * LANES:(pair + 1) * LANES] = jnp.where(low, left, heads[2 * pair + 1]).astype(BF16)


def _overlap_matrix(nb, nsbp):
    cs = np.arange(nb)[:, None] * CMP_STRIDE
    ss = np.arange(nsbp)[None, :] * SEL_BLOCK
    ov = np.minimum(cs + CMP_BLOCK, ss + SEL_BLOCK) - np.maximum(cs, ss)
    return jnp.asarray(np.maximum(ov, 0).astype(np.float32) / CMP_BLOCK, dtype=BF16)


def _nsa_attention(rel_bias, q, gates, cmpkv, ka, vot, wk, wvt, *, n_real, q_pos0, win_pos0, n_sel_rows,
                   skip_empty):
    b, tp, _ = q.shape
    nb = cmpkv.shape[1]
    ls = ka.shape[1]
    lw = wk.shape[1]
    tq = LANES
    n_sb = -(-n_sel_rows // SEL_BLOCK)
    nsbp = _round_up(n_sb, LANES)
    assert q_pos0 % SEL_TILE == 0 and win_pos0 % KEY_TILE == 0 and n_sb >= SEL_TOPN
    assert ls % SEL_TILE == 0 and lw % KEY_TILE == 0 and tp % tq == 0 and vot.shape[2] == ls and wvt.shape[2] == lw
    ov = _overlap_matrix(nb, nsbp)
    kern = functools.partial(_nsa_kernel, tq=tq, n_real=n_real, q_pos0=q_pos0, win_pos0=win_pos0, nb=nb,
                             nsbp=nsbp, n_sb=n_sb, ls=ls, lw=lw, skip_empty=skip_empty)
    kb_first = (q_pos0 - win_pos0) // KEY_TILE - (WIN_TILES - 1)
    win_blk = lambda t, i: jnp.clip(kb_first + t + i, 0, lw // KEY_TILE - 1)
    wk_specs = [pl.BlockSpec((1, KEY_TILE, LANES), lambda g, s, t, i=i: (s, win_blk(t, i), g))
                for i in range(WIN_TILES)]
    wvt_specs = [pl.BlockSpec((1, LANES, KEY_TILE), lambda g, s, t, i=i: (s, g, win_blk(t, i)))
                 for i in range(WIN_TILES)]
    rows_all = NSA_REP * tq
    return pl.pallas_call(
        kern,
        grid=(NSA_GROUPS, b, tp // tq),
        in_specs=[pl.BlockSpec(memory_space=pltpu.SMEM),
                  pl.BlockSpec((1, tq, NSA_REP * LANES), lambda g, i, t: (i, t, g)),
                  pl.BlockSpec((1, 1, tq, LANES), lambda g, i, t: (i, g, t, 0)),
                  pl.BlockSpec((1, nb, LANES), lambda g, i, t: (i, 0, g)),
                  pl.BlockSpec((1, ls, LANES), lambda g, i, t: (i, 0, g)),
                  pl.BlockSpec((1, LANES, ls), lambda g, i, t: (i, g, 0))]
        + wk_specs + wvt_specs + [pl.BlockSpec((nb, nsbp), lambda g, i, t: (0, 0))],
        out_specs=pl.BlockSpec((1, tq, NSA_REP * HEAD_DIM), lambda g, i, t: (i, t, g)),
        out_shape=jax.ShapeDtypeStruct((b, tp, NSA_WIDTH), BF16),
        scratch_shapes=[pltpu.VMEM((N_BIAS_TILES, NSA_REP, KEY_TILE, tq), F32),
                        pltpu.VMEM((NSA_REP, tq, CMP_TAB_WIDTH), F32),
                        pltpu.VMEM((NSA_REP, WIN_TILES * KEY_TILE, tq), F32),
                        pltpu.VMEM((rows_all, nb), F32),
                        pltpu.VMEM((nsbp, tq), F32),
                        pltpu.VMEM((LANES, rows_all), F32),
                        pltpu.VMEM((WIN_TILES * KEY_TILE, rows_all), F32),
                        pltpu.VMEM((WIN_TILES * KEY_TILE, rows_all), BF16),
                        pltpu.VMEM((2, SEL_TILE, rows_all), F32),
                        pltpu.VMEM((2, SEL_TILE, rows_all), BF16),
                        pltpu.VMEM((LANES, rows_all), F32),
                        pltpu.VMEM((1, rows_all), F32)],
        compiler_params=_cparams(("arbitrary", "arbitrary", "arbitrary")),
        name="nsa_attention",
    )(rel_bias, q, gates, cmpkv, ka, vot, *([wk] * WIN_TILES), *([wvt] * WIN_TILES), ov)


def _sb_tile(q, k, v, u, causal, carry):
    z = _dot_nt(q, k) * (SB_HEAD_DIM ** -0.5)
    softplus = jnp.maximum(z, 0.0) + jnp.log(1.0 + jnp.exp(-jnp.abs(z)))
    log_stay = -softplus if causal is None else jnp.where(causal, -softplus, 0.0)
    hi = log_stay.astype(BF16)
    lo = (log_stay - hi.astype(F32)).astype(BF16)
    sums = _dot(jnp.concatenate([hi, lo], axis=1), u)
    a = jnp.exp((z - softplus) + sums[:, :LANES] + carry)
    if causal is not None:
        a = jnp.where(causal, a, 0.0)
    return _dot(a.astype(BF16), v), carry + sums[:, LANES:]


def _sb_kernel(q_ref, k_ref, v_ref, u_ref, o_ref, acc_ref, car_ref, *, tq, q_pos0, lp):
    qt = pl.program_id(2)
    t0 = q_pos0 + qt * tq
    q = q_ref[0]
    rows = lax.broadcasted_iota(jnp.int32, (tq, LANES), 0)
    cols = lax.broadcasted_iota(jnp.int32, (tq, LANES), 1)
    tpos = t0 + rows
    acc_ref[...] = jnp.zeros(acc_ref.shape, F32)
    car_ref[...] = jnp.zeros(car_ref.shape, F32)

    def cond(c):
        kt, go = c
        return (kt >= 0) & go

    def body(c):
        kt, _ = c
        k0 = pl.multiple_of(kt * KEY_TILE, KEY_TILE)
        pv, carry = _sb_tile(q, k_ref[0, pl.ds(k0, KEY_TILE), :], v_ref[0, pl.ds(k0, KEY_TILE), :], u_ref[...],
                             k0 + cols < tpos, car_ref[...])
        acc_ref[...] += pv
        car_ref[...] = carry
        return kt - 1, jnp.max(carry) > EXP_UNDERFLOW

    kt_hi = jnp.minimum(jnp.right_shift(t0 + tq - 2, 7), lp // KEY_TILE - 1)
    lax.while_loop(cond, body, (kt_hi, True))
    o_ref[0] = acc_ref[...].astype(BF16)


def _suffix_matrix():
    j = np.arange(2 * KEY_TILE)[:, None] % KEY_TILE
    s = np.arange(2 * KEY_TILE)[None, :]
    u = np.where(s < KEY_TILE, j > s, True)
    return jnp.asarray(u.astype(np.float32), dtype=BF16)


def _sb_attention(yb, off, *, tq):
    _, t, _ = yb.shape
    assert t % KEY_TILE == 0 and t % tq == 0 and tq >= 2
    cq = (off + _OFF_QB) // SB_HEAD_DIM
    ck = (off + _OFF_KVB) // SB_HEAD_DIM
    cv = ck + SB_HEADS
    return pl.pallas_call(
        functools.partial(_sb_kernel, tq=tq, q_pos0=0, lp=t),
        grid=(1, SB_HEADS, t // tq),
        in_specs=[pl.BlockSpec((1, tq, SB_HEAD_DIM), lambda i, h, s: (i, s, cq + h)),
                  pl.BlockSpec((1, t, SB_HEAD_DIM), lambda i, h, s: (i, 0, ck + h)),
                  pl.BlockSpec((1, t, SB_HEAD_DIM), lambda i, h, s: (i, 0, cv + h)),
                  pl.BlockSpec((2 * KEY_TILE, 2 * KEY_TILE), lambda i, h, s: (0, 0))],
        out_specs=pl.BlockSpec((1, tq, SB_HEAD_DIM), lambda i, h, s: (i, s, h)),
        out_shape=jax.ShapeDtypeStruct((1, t, SB_WIDTH), BF16),
        scratch_shapes=[pltpu.VMEM((tq, SB_HEAD_DIM), F32), pltpu.VMEM((tq, LANES), F32)],
        compiler_params=_cparams(("arbitrary", "arbitrary", "arbitrary")),
        name="sb_attention",
    )(yb, yb, yb, _suffix_matrix())


def _sb_decode_kernel(pt_ref, q_ref, u_ref, cache_ref, o_ref, buf_ref, sem_ref, acc_ref, car_ref, *, n_pages):
    s = pl.program_id(0)
    acc_ref[...] = jnp.zeros(acc_ref.shape, F32)
    car_ref[...] = jnp.zeros(car_ref.shape, F32)

    def page_copy(j, slot):
        return pltpu.make_async_copy(cache_ref.at[pt_ref[s, j]], buf_ref.at[slot], sem_ref.at[slot])

    def slot_of(j):
        return (n_pages - 1 - j) % 2

    page_copy(n_pages - 1, 0).start()

    def cond(c):
        j, go = c
        return (j >= 0) & go

    def body(c):
        j, _ = c
        slot = slot_of(j)

        @pl.when(j > 0)
        def _():
            page_copy(j - 1, 1 - slot).start()

        page_copy(j, slot).wait()
        mx = jnp.float32(2.0 * NEG_INF)
        for h in range(SB_HEADS):
            k = buf_ref[slot, pl.ds(h, PAGE_SIZE, stride=2 * SB_HEADS), :].astype(BF16)
            v = buf_ref[slot, pl.ds(SB_HEADS + h, PAGE_SIZE, stride=2 * SB_HEADS), :].astype(BF16)
            pv, carry = _sb_tile(q_ref[0, h], k, v, u_ref[...], None, car_ref[h])
            acc_ref[h] += pv
            car_ref[h] = carry
            mx = jnp.maximum(mx, jnp.max(carry[0:1, :]))
        return j - 1, mx > EXP_UNDERFLOW

    j_end, _ = lax.while_loop(cond, body, (n_pages - 1, True))

    @pl.when(j_end >= 0)
    def _():
        page_copy(j_end, slot_of(j_end)).wait()

    for h in range(SB_HEADS):
        o_ref[0, h:h + 1, :] = acc_ref[h][0:1, :]


def _sb_decode(cache_l, page_table, q):
    b, n_pages = page_table.shape
    rows = cache_l.shape[1]
    qr = q.shape[2]
    return pl.pallas_call(
        functools.partial(_sb_decode_kernel, n_pages=n_pages),
        grid_spec=pltpu.PrefetchScalarGridSpec(
            num_scalar_prefetch=1, grid=(b,),
            in_specs=[pl.BlockSpec((1, SB_HEADS, qr, SB_HEAD_DIM), lambda s, pt: (s, 0, 0, 0)),
                      pl.BlockSpec((2 * KEY_TILE, 2 * KEY_TILE), lambda s, pt: (0, 0)),
                      pl.BlockSpec(memory_space=pl.ANY)],
            out_specs=pl.BlockSpec((1, SB_HEADS, SB_HEAD_DIM), lambda s, pt: (s, 0, 0)),
            scratch_shapes=[pltpu.VMEM((2, rows, SB_HEAD_DIM), F32), pltpu.SemaphoreType.DMA((2,)),
                            pltpu.VMEM((SB_HEADS, qr, SB_HEAD_DIM), F32), pltpu.VMEM((SB_HEADS, qr, LANES), F32)]),
        out_shape=jax.ShapeDtypeStruct((b, SB_HEADS, SB_HEAD_DIM), F32),
        compiler_params=_cparams(("arbitrary",)),
        name="sb_decode",
    )(page_table, q, _suffix_matrix(), cache_l)


def _finish_kernel(oa_ref, ob_ref, gma_ref, gmb_ref, x_ref, wa_ref, wb_ref, wo_ref, g1_ref, b1_ref,
                   wr_ref, br_ref, h_ref, comb_ref, *, alpha, n_experts):
    ua = _dot(oa_ref[...], wa_ref[...])
    ub = _dot(ob_ref[...], wb_ref[...])
    mixed = jax.nn.sigmoid(gma_ref[...]) * ua + jax.nn.sigmoid(gmb_ref[...]) * ub
    mo = _dot(mixed.astype(BF16), wo_ref[...])
    h = _layer_norm(alpha * x_ref[...] + mo, g1_ref[...], b1_ref[...])
    h_ref[...] = h
    logits = _dot(h.astype(BF16), wr_ref[...]) + br_ref[...]
    lane = lax.broadcasted_iota(jnp.int32, logits.shape, 1)
    lane_f = lane.astype(F32)
    sc = jnp.where(lane < n_experts, logits, NEG_INF)
    vals, picks = [], []
    for _ in range(TOP_K):
        mx = jnp.max(sc, axis=1, keepdims=True)
        first = jnp.min(jnp.where(sc == mx, lane_f, 1e9), axis=1, keepdims=True)
        pick = lane_f == first
        vals.append(mx)
        picks.append(pick)
        sc = jnp.where(pick, 2.0 * NEG_INF, sc)
    es = [jnp.exp(v - vals[0]) for v in vals]
    den = es[0] + es[1] + es[2] + es[3]
    comb = jnp.zeros(logits.shape, F32)
    for k in range(TOP_K):
        comb = jnp.where(picks[k], es[k] / den, comb)
    comb_ref[...] = comb


def _finish(o_a, o_b, y, x, weights, *, tm, d_model, alpha, n_experts):
    wa, wb, wo, g1, b1, wr, br = weights
    m = x.shape[0]
    full = lambda a: pl.BlockSpec(a.shape, lambda i: (0, 0), pipeline_mode=pl.Buffered(1))
    return pl.pallas_call(
        functools.partial(_finish_kernel, alpha=alpha, n_experts=n_experts),
        grid=(m // tm,),
        in_specs=[pl.BlockSpec((tm, NSA_WIDTH), lambda i: (i, 0)),
                  pl.BlockSpec((tm, SB_WIDTH), lambda i: (i, 0)),
                  pl.BlockSpec((tm, d_model), lambda i: (i, 0)),
                  pl.BlockSpec((tm, d_model), lambda i: (i, 1)),
                  pl.BlockSpec((tm, d_model), lambda i: (i, 0)),
                  full(wa), full(wb), full(wo), full(g1), full(b1), full(wr), full(br)],
        out_specs=[pl.BlockSpec((tm, d_model), lambda i: (i, 0)),
                   pl.BlockSpec((tm, LANES), lambda i: (i, 0))],
        out_shape=[jax.ShapeDtypeStruct((m, d_model), F32), jax.ShapeDtypeStruct((m, LANES), F32)],
        compiler_params=_cparams(("arbitrary",)),
        name="out_proj_ln_route",
    )(o_a, o_b, y, y, x, wa, wb, wo, g1, b1, wr, br)


def _split_gate_up_kernel(w_ref, pg_ref, pu_ref, g_ref, u_ref):
    w = w_ref[0].astype(BF16)
    g_ref[0] = _dot(w, pg_ref[...]).astype(BF16)
    u_ref[0] = _dot(w, pu_ref[...]).astype(BF16)


def _split_gate_up(w_gate_up_l):
    e, d, f2 = w_gate_up_l.shape
    tk = min(512, d)
    pg = np.zeros((f2, f2 // 2), np.float32)
    pu = np.zeros((f2, f2 // 2), np.float32)
    pg[2 * np.arange(f2 // 2), np.arange(f2 // 2)] = 1.0
    pu[2 * np.arange(f2 // 2) + 1, np.arange(f2 // 2)] = 1.0
    out = jax.ShapeDtypeStruct((e, d, f2 // 2), BF16)
    return pl.pallas_call(
        _split_gate_up_kernel,
        grid=(e, d // tk),
        in_specs=[pl.BlockSpec((1, tk, f2), lambda i, k: (i, k, 0)),
                  pl.BlockSpec(pg.shape, lambda i, k: (0, 0)),
                  pl.BlockSpec(pu.shape, lambda i, k: (0, 0))],
        out_specs=[pl.BlockSpec((1, tk, f2 // 2), lambda i, k: (i, k, 0))] * 2,
        out_shape=[out, out],
        compiler_params=_cparams(("arbitrary", "arbitrary")),
        name="moe_split_gate_up",
    )(w_gate_up_l, jnp.asarray(pg, dtype=BF16), jnp.asarray(pu, dtype=BF16))


def _moe_kernel(h_ref, comb_ref, wg_ref, wu_ref, bg_ref, bu_ref, wd_ref, bd_ref, g2_ref, b2_ref, o_ref,
                hb_ref, acc_ref, *, alpha, n_experts):
    e = pl.program_id(1)

    @pl.when(e == 0)
    def _():
        hb_ref[...] = h_ref[...].astype(BF16)
        acc_ref[...] = jnp.zeros_like(acc_ref)

    hb = hb_ref[...]
    gate = jnp.minimum(_dot(hb, wg_ref[0]) + bg_ref[0], SWIGLU_LIMIT)
    up = jnp.clip(_dot(hb, wu_ref[0]) + bu_ref[0], -SWIGLU_LIMIT, SWIGLU_LIMIT)
    act = (up + 1.0) * gate * jax.nn.sigmoid(SWIGLU_ALPHA * gate)
    y = _dot(act.astype(BF16), wd_ref[0]) + bd_ref[0]
    lane = lax.broadcasted_iota(jnp.int32, comb_ref.shape, 1)
    c = jnp.sum(jnp.where(lane == e, comb_ref[...], 0.0), axis=1, keepdims=True)
    acc_ref[...] += c * y

    @pl.when(e == n_experts - 1)
    def _():
        o_ref[...] = _layer_norm(alpha * h_ref[...] + acc_ref[...], g2_ref[...], b2_ref[...])


def _moe(h, comb, weights, *, tm, alpha):
    wg, wu, bg, bu, wd, bd, g2, b2 = weights
    m, d_model = h.shape
    n_experts, _, d_ff = wg.shape
    return pl.pallas_call(
        functools.partial(_moe_kernel, alpha=alpha, n_experts=n_experts),
        grid=(m // tm, n_experts),
        in_specs=[pl.BlockSpec((tm, d_model), lambda i, e: (i, 0)),
                  pl.BlockSpec((tm, LANES), lambda i, e: (i, 0)),
                  pl.BlockSpec((1, d_model, d_ff), lambda i, e: (e, 0, 0)),
                  pl.BlockSpec((1, d_model, d_ff), lambda i, e: (e, 0, 0)),
                  pl.BlockSpec((1, 1, d_ff), lambda i, e: (e, 0, 0)),
                  pl.BlockSpec((1, 1, d_ff), lambda i, e: (e, 0, 0)),
                  pl.BlockSpec((1, d_ff, d_model), lambda i, e: (e, 0, 0)),
                  pl.BlockSpec((1, 1, d_model), lambda i, e: (e, 0, 0)),
                  pl.BlockSpec((1, d_model), lambda i, e: (0, 0)),
                  pl.BlockSpec((1, d_model), lambda i, e: (0, 0))],
        out_specs=pl.BlockSpec((tm, d_model), lambda i, e: (i, 0)),
        out_shape=jax.ShapeDtypeStruct((m, d_model), F32),
        scratch_shapes=[pltpu.VMEM((tm, d_model), BF16), pltpu.VMEM((tm, d_model), F32)],
        compiler_params=_cparams(("arbitrary", "arbitrary")),
        name="moe_ln",
    )(h, comb, wg, wu, bg, bu, wd, bd, g2, b2)


def _gather_cmp_kernel(pt_ref, *refs, n_steps):
    page_refs = refs[:PAGES_PER_STEP]
    new_ref, o_ref, rows_ref = refs[PAGES_PER_STEP:]
    j = pl.program_id(1)

    n_lane_tiles = NSA_KV_WIDTH // LANES

    @pl.when(j < n_steps)
    def _():
        for i in range(PAGES_PER_STEP):
            x = page_refs[i][0].T
            for c in range(n_lane_tiles):
                rows_ref[c, i * PAGE_SIZE:(i + 1) * PAGE_SIZE, :] = x[:, c * LANES:(c + 1) * LANES]

    @pl.when(j == n_steps)
    def _():
        rows_ref[...] = jnp.zeros(rows_ref.shape, F32)
        for c in range(n_lane_tiles):
            rows_ref[c, 0:8, :] = new_ref[0, :, c * LANES:(c + 1) * LANES]

    n_chunks = PAGES_PER_STEP * PAGE_SIZE // CMP_STRIDE
    for p in range(CMP_STRIDE):
        for c in range(n_lane_tiles):
            o_ref[0, :, p * NSA_KV_WIDTH + c * LANES:p * NSA_KV_WIDTH + (c + 1) * LANES] = rows_ref[
                c, pl.ds(p, n_chunks, stride=CMP_STRIDE), :].astype(BF16)


def _gather_sel_kernel(pt_ref, *refs, n_steps):
    page_refs = refs[:PAGES_PER_STEP]
    newk_ref, newv_ref, ka_ref, vot_ref = refs[PAGES_PER_STEP:]
    j = pl.program_id(1)
    tag_row = lax.broadcasted_iota(jnp.int32, (HEAD_DIM, PAGE_SIZE), 0)
    tag_col = jnp.right_shift(lax.broadcasted_iota(jnp.int32, (HEAD_DIM, PAGE_SIZE), 1), 6)
    pages_per_tile = SEL_TILE // PAGE_SIZE

    @pl.when(j < n_steps)
    def _():
        for i in range(PAGES_PER_STEP):
            xt = page_refs[i][0]
            tag = jnp.where(tag_row == (PAGE_SIZE // SEL_BLOCK) * (i % pages_per_tile) + tag_col, 1.0, 0.0)
            rs = slice(i * PAGE_SIZE, (i + 1) * PAGE_SIZE)
            for g in range(NSA_GROUPS):
                kt = xt[g * HEAD_DIM:(g + 1) * HEAD_DIM, :]
                vt = xt[(NSA_GROUPS + g) * HEAD_DIM:(NSA_GROUPS + g + 1) * HEAD_DIM, :]
                ka_ref[0, rs, g * LANES:(g + 1) * LANES] = jnp.concatenate([kt, tag], axis=0).T.astype(BF16)
                vot_ref[0, g * LANES:g * LANES + HEAD_DIM, rs] = vt.astype(BF16)
                vot_ref[0, g * LANES + HEAD_DIM:(g + 1) * LANES, rs] = jnp.ones((HEAD_DIM, PAGE_SIZE), BF16)

    @pl.when(j == n_steps)
    def _():
        ka_ref[...] = jnp.zeros(ka_ref.shape, BF16)
        vot_ref[...] = jnp.zeros(vot_ref.shape, BF16)
        ka_ref[0, 0:16, :] = newk_ref[0]
        vnew = newv_ref[0].astype(F32)
        vnew = jnp.concatenate([vnew, jnp.zeros((LANES - vnew.shape[0], NSA_PACK_WIDTH), F32)], axis=0)
        vot_ref[0, :, 0:LANES] = vnew.T.astype(BF16)


def _gather_nsa(cache_t, page_table, new_blocks, kernel_fn, outs, scratch, name):
    b, n_pages = page_table.shape
    assert n_pages % PAGES_PER_STEP == 0
    n_steps = n_pages // PAGES_PER_STEP

    def page_spec(i):
        return pl.BlockSpec((1, NSA_KV_WIDTH, PAGE_SIZE),
                            lambda s, j, pt: (pt[s, jnp.minimum(j * PAGES_PER_STEP + i, n_pages - 1)], 0, 0))

    def out_spec(shape, axis):
        return pl.BlockSpec((1,) + shape, lambda s, j, pt: (s, j, 0) if axis == 0 else (s, 0, j))

    def out_shape(shape, axis):
        full = list(shape)
        full[axis] *= n_steps + 1
        return jax.ShapeDtypeStruct((b,) + tuple(full), BF16)

    in_specs = [page_spec(i) for i in range(PAGES_PER_STEP)]
    in_specs += [pl.BlockSpec((1,) + nb.shape[1:], lambda s, j, pt: (s, 0, 0)) for nb in new_blocks]
    return pl.pallas_call(
        functools.partial(kernel_fn, n_steps=n_steps),
        grid_spec=pltpu.PrefetchScalarGridSpec(
            num_scalar_prefetch=1, grid=(b, n_steps + 1), in_specs=in_specs,
            out_specs=[out_spec(s, a) for s, a in outs], scratch_shapes=scratch),
        out_shape=[out_shape(s, a) for s, a in outs],
        compiler_params=_cparams(("arbitrary", "arbitrary")),
        name=name,
    )(page_table, *([cache_t] * PAGES_PER_STEP), *new_blocks)


def _nsa_cache_pages(cache_l):
    n_pool = cache_l.shape[0]
    return cache_l.transpose(0, 2, 3, 4, 1).reshape(n_pool, NSA_KV_WIDTH, PAGE_SIZE)


def _nsa_gates(y, off):
    b, t, _ = y.shape
    g = y[:, :, off + _OFF_GA:off + _OFF_GA + 3 * NSA_HEADS].reshape(b, t, 3, NSA_GROUPS, NSA_REP)
    g = g.transpose(0, 3, 1, 2, 4).reshape(b, NSA_GROUPS, t, 3 * NSA_REP)
    return jnp.pad(g, ((0, 0), (0, 0), (0, 0), (0, LANES - 3 * NSA_REP)))


def kernel(x_prompt, x_sample, cache_cmp_kv, cache_sel_kv, cache_sb_kv, state_win_kv, page_table, rel_bias,
           w_in, cmp_pe, cmp_w1, cmp_w2, w_up_nsa, w_up_sb, w_out, ln1_g, ln1_b,
           w_router, b_router, w_gate_up, b_gate_up, w_down, b_down, ln2_g, ln2_b):
    depth, d_model = w_in.shape[0], w_in.shape[1]
    bp, seq, _ = x_prompt.shape
    bs, dec_seq, _ = x_sample.shape
    assert bp == 1 and dec_seq == 1
    n_pages = page_table.shape[1]
    past = n_pages * PAGE_SIZE
    n_buf = state_win_kv.shape[2]
    n_experts = w_router.shape[2]
    n_pool = cache_cmp_kv.shape[1]
    alpha = (2 * depth) ** 0.25
    off = 2 * d_model
    tq_s = 16
    assert seq % SEL_TILE == 0 and past % SEL_TILE == 0

    hp = x_prompt.reshape(seq, d_model)
    hs = x_sample.reshape(bs, d_model)
    outs = [[] for _ in range(8)]
    for l in range(depth):
        w_proj = _proj_weight(w_in[l], d_model)
        cmp_w = _compress_weights(cmp_pe[l], cmp_w1[l], cmp_w2[l])
        fin_w = (w_up_nsa[l].astype(BF16), w_up_sb[l].astype(BF16), w_out[l].astype(BF16),
                 ln1_g[l][None], ln1_b[l][None],
                 jnp.pad(w_router[l], ((0, 0), (0, LANES - n_experts))).astype(BF16),
                 jnp.pad(b_router[l], (0, LANES - n_experts))[None])
        w_gate, w_upp = _split_gate_up(w_gate_up[l])
        moe_w = (w_gate, w_upp, b_gate_up[l][:, None, 0::2], b_gate_up[l][:, None, 1::2],
                 w_down[l].astype(BF16), b_down[l][:, None, :], ln2_g[l][None], ln2_b[l][None])

        tm = min(512, seq)
        y, yb = _in_proj(hp, w_proj, min(PROJ_TM, seq), PROJ_TN)
        kv_c = y[:, off + _OFF_KVC:off + _OFF_KVC + NSA_KV_WIDTH]
        kv_s = y[:, off + _OFF_KVS:off + _OFF_KVS + NSA_KV_WIDTH]
        kv_w = y[:, off + _OFF_KVW:off + _OFF_KVW + NSA_KV_WIDTH]
        kv_b = y[:, off + _OFF_KVB:off + _OFF_KVB + 2 * SB_WIDTH]
        q_pk, ka, vot, wk, wvt = _pack(yb, off, tm, 0, 1, True)
        n_chunks = seq // CMP_STRIDE
        chunks = yb[:, off + _OFF_KVC:off + _OFF_KVC + NSA_KV_WIDTH].reshape(1, n_chunks, CHUNK_WIDTH)
        cmpkv = _compress(chunks, seq, cmp_w)
        o_a = _nsa_attention(rel_bias, q_pk[None], _nsa_gates(y[None], off), cmpkv, ka[None], vot[None],
                             wk[None], wvt[None], n_real=KEY_TILE, q_pos0=0, win_pos0=0, n_sel_rows=seq,
                             skip_empty=False)
        o_b = _sb_attention(yb[None], off, tq=SB_PROMPT_TQ)
        h1, comb = _finish(o_a[0], o_b[0], y, hp, fin_w, tm=min(256, seq), d_model=d_model, alpha=alpha,
                           n_experts=n_experts)
        hp = _moe(h1, comb, moe_w, tm=min(512, seq), alpha=alpha)
        n_win = min(WINDOW, seq)
        outs[0].append(kv_c.reshape(1, seq, 2, NSA_GROUPS, HEAD_DIM))
        outs[1].append(kv_s.reshape(1, seq, 2, NSA_GROUPS, HEAD_DIM))
        outs[2].append(kv_b.reshape(1, seq, 2, SB_HEADS, SB_HEAD_DIM))
        outs[3].append(kv_w[seq - n_win:].reshape(1, n_win, 2, NSA_GROUPS, HEAD_DIM))

        ys, ysb = _in_proj(hs, w_proj, bs, PROJ_TN)
        kv_c = ys[:, off + _OFF_KVC:off + _OFF_KVC + NSA_KV_WIDTH]
        kv_s = ys[:, off + _OFF_KVS:off + _OFF_KVS + NSA_KV_WIDTH]
        kv_w = ys[:, off + _OFF_KVW:off + _OFF_KVW + NSA_KV_WIDTH]
        kv_b = ys[:, off + _OFF_KVB:off + _OFF_KVB + 2 * SB_WIDTH]
        q_pk, ka_new, vo_new, _, _ = _pack(ysb, off, bs, past, 0, False)
        rows_step = PAGES_PER_STEP * PAGE_SIZE
        (chunks,) = _gather_nsa(_nsa_cache_pages(cache_cmp_kv[l]), page_table,
                                [_pad_rows(kv_c[:, None, :], 1, 8)], _gather_cmp_kernel,
                                [((rows_step // CMP_STRIDE, CHUNK_WIDTH), 0)],
                                [pltpu.VMEM((NSA_KV_WIDTH // LANES, rows_step, LANES), F32)], "gather_cmp")
        ka, vot = _gather_nsa(_nsa_cache_pages(cache_sel_kv[l]), page_table,
                              [_pad_rows(ka_new[:, None, :], 1, 16), _pad_rows(vo_new[:, None, :], 1, 16)],
                              _gather_sel_kernel, [((rows_step, NSA_PACK_WIDTH), 0), ((NSA_PACK_WIDTH, rows_step), 1)],
                              [], "gather_sel")
        cmpkv = _compress(chunks, past + 1, cmp_w)
        win_all = jnp.concatenate([state_win_kv[l].reshape(bs, n_buf, NSA_KV_WIDTH), kv_w[:, None, :]], axis=1)
        lw = _round_up(n_buf + 1, KEY_TILE)
        win5 = win_all.reshape(bs, n_buf + 1, 2, NSA_GROUPS, HEAD_DIM).astype(BF16)
        wk = jnp.pad(win5[:, :, 0], ((0, 0), (0, lw - n_buf - 1), (0, 0), (0, LANES - HEAD_DIM)))
        wk = wk.reshape(bs, lw, NSA_PACK_WIDTH)
        wvt = jnp.concatenate([win5[:, :, 1], jnp.ones_like(win5[:, :, 1])], axis=-1)
        wvt = _pad_rows(wvt.transpose(0, 2, 3, 1).reshape(bs, NSA_PACK_WIDTH, n_buf + 1), 2, lw)
        o_a = _nsa_attention(rel_bias, _pad_rows(q_pk[:, None, :], 1, KEY_TILE),
                             _pad_rows(_nsa_gates(ys[:, None, :], off), 2, KEY_TILE),
                             cmpkv, ka, vot, wk, wvt, n_real=1, q_pos0=past, win_pos0=past - n_buf,
                             n_sel_rows=past + 1, skip_empty=True)
        q_b = ysb[:, off + _OFF_QB:off + _OFF_QB + SB_WIDTH].reshape(bs, SB_HEADS, 1, SB_HEAD_DIM)
        o_b = _sb_decode(cache_sb_kv[l].reshape(n_pool, PAGE_SIZE * 2 * SB_HEADS, SB_HEAD_DIM), page_table,
                         _pad_rows(q_b, 2, tq_s))
        h1, comb = _finish(o_a[:, 0], o_b.reshape(bs, SB_WIDTH).astype(BF16), ys, hs, fin_w, tm=bs,
                           d_model=d_model, alpha=alpha, n_experts=n_experts)
        hs = _moe(h1, comb, moe_w, tm=bs, alpha=alpha)
        outs[4].append(kv_c.reshape(bs, 1, 2, NSA_GROUPS, HEAD_DIM))
        outs[5].append(kv_s.reshape(bs, 1, 2, NSA_GROUPS, HEAD_DIM))
        outs[6].append(kv_b.reshape(bs, 1, 2, SB_HEADS, SB_HEAD_DIM))
        outs[7].append(win_all[:, 1:].reshape(bs, n_buf, 2, NSA_GROUPS, HEAD_DIM))

    return (hp.reshape(bp, seq, d_model), hs.reshape(bs, dec_seq, d_model)) + tuple(jnp.stack(o) for o in outs)
```
